```python
import math
import jax
import jax.numpy as jnp
from jax import lax
import numpy as np

D_MODEL = 1024
BATCH = 8
SEQ = 4096
DEPTH = 2

GRID_W = 64
CTX_LEN = 256
N_EVEN = (DEPTH + 1) // 2
N_ODD = DEPTH // 2
N_MOD = 9
D_FF = 2816
NORM_EPS = 1e-6
MIX_WIDTH = D_MODEL
A_WIDTH = MIX_WIDTH // 2
A_HEADS = 4
A_QK_DIM = A_WIDTH // (2 * A_HEADS)
A_V_DIM = 2 * A_QK_DIM
B_WIDTH = MIX_WIDTH - A_WIDTH
AB_IN = 3 * A_WIDTH + 3 * B_WIDTH
HY_IN = 3 * B_WIDTH
HY_ORDER = 2
HY_SHORT = 3
HY_EMB = 33
HY_BANDS = (HY_EMB - 1) // 2
HY_FILTER_HID = 64
HY_INNER = 2
HY_FILTER_STD = 0.03
HY_TARGET = 1e-2
HY_MAX_DECAY = math.log(HY_TARGET) / 0.3
HY_MIN_DECAY = math.log(HY_TARGET) / 1.5
C_WIDTH = MIX_WIDTH
C_HEADS = 16
C_HEAD_DIM = C_WIDTH // C_HEADS
NA_ROWS_MAX = 8
NA_COLS = 16
ROPE_BASE = 10000.0
QBLOCK = 128

kernel_name = 'hybrid_diffattn_hyena_natten_macaron_dit'


def rmsnorm(x, w):
    xf = x.astype(jnp.float32)
    y = xf * lax.rsqrt(jnp.mean(xf * xf, axis=-1, keepdims=True) + NORM_EPS)
    return (y * w.astype(jnp.float32)).astype(x.dtype)


def adaln_in(h, gain, mod, k):
    return rmsnorm(h, gain) * (1.0 + mod[:, 3 * k + 1]) + mod[:, 3 * k]


def swiglu(h, w1, w3, w2):
    return (jax.nn.silu(h @ w1) * (h @ w3)) @ w2


def macaron(h, gain, mod, k, w1, w3, w2):
    return h + 0.5 * mod[:, 3 * k + 2] * swiglu(adaln_in(h, gain, mod, k), w1, w3, w2)


def axial_rope(x):
    n = x.shape[1]
    t = jnp.arange(n)
    pos = (t // GRID_W, t % GRID_W)
    half = x.shape[-1] // 2
    nf = half // 2
    inv_freq = ROPE_BASE ** (-jnp.arange(nf, dtype=jnp.float32) / nf)
    bshape = (1, n) + (1,) * (x.ndim - 3) + (nf,)
    parts = []
    for a in range(2):
        ang = pos[a].astype(jnp.float32)[:, None] * inv_freq
        cos = jnp.cos(ang).reshape(bshape).astype(x.dtype)
        sin = jnp.sin(ang).reshape(bshape).astype(x.dtype)
        xa = x[..., a * half:(a + 1) * half]
        x1, x2 = xa[..., :nf], xa[..., nf:]
        parts += [x1 * cos - x2 * sin, x1 * sin + x2 * cos]
    return jnp.concatenate(parts, axis=-1)


def diff_attend(q, k, v, lam):
    s = jnp.einsum('bqhmd,bkhmd->bhmqk', q, k).astype(jnp.float32) * (A_QK_DIM ** -0.5)
    p = jax.nn.softmax(s, axis=-1)
    a = (p[:, :, 0] - lam * p[:, :, 1]).astype(v.dtype)
    return jnp.einsum('bhqk,bkhd->bqhd', a, v)


def plain_attend(q, k, v):
    s = jnp.einsum('bqhd,bkhd->bhqk', q, k).astype(jnp.float32) * (q.shape[-1] ** -0.5)
    p = jax.nn.softmax(s, axis=-1).astype(v.dtype)
    return jnp.einsum('bhqk,bkhd->bqhd', p, v)


def short_conv(u, w, b):
    n = u.shape[1]
    pad = HY_SHORT // 2
    up = jnp.pad(u, ((0, 0), (pad, pad), (0, 0)))
    return sum(up[:, j:j + n] * w[j] for j in range(HY_SHORT)) + b


def hyena_filters(n, f_w0, f_b0, f_w1, f_b1, f_freq, f_wout):
    t = jnp.linspace(0.0, 1.0, n, dtype=jnp.float32)[:, None]
    w = (2.0 * math.pi / n) * jnp.arange(n, dtype=jnp.float32)[:, None]
    bands = jnp.linspace(1e-4, HY_BANDS - 1, HY_BANDS, dtype=jnp.float32)[None, :]
    z = jnp.concatenate([t, jnp.cos(bands * w), -jnp.sin(bands * w)], axis=-1)
    hid = jnp.sin(f_freq * (z @ f_w0 + f_b0))
    for i in range(HY_INNER):
        hid = jnp.sin(f_freq * (hid @ f_w1[i] + f_b1[i]))
    filt = (hid @ f_wout).reshape(n, HY_ORDER, 2, B_WIDTH)
    deltas = jnp.abs(jnp.linspace(HY_MIN_DECAY, HY_MAX_DECAY, B_WIDTH, dtype=jnp.float32))
    window = jnp.exp(-t * deltas)
    return filt * window[:, None, None, :]


def long_conv_bidir(v, h_fwd, h_bwd, skip):
    n = v.shape[1]
    taps = jnp.concatenate([h_fwd, jnp.zeros_like(h_fwd[:1]), h_bwd[:0:-1]], axis=0)
    taps_f = jnp.fft.rfft(taps.astype(jnp.float32), n=2 * n, axis=0)
    v_f = jnp.fft.rfft(v.astype(jnp.float32), n=2 * n, axis=1)
    y = jnp.fft.irfft(v_f * taps_f[None], n=2 * n, axis=1)[:, :n]
    return (y + v.astype(jnp.float32) * skip.astype(jnp.float32)).astype(v.dtype)


def hyena(u, conv_w, conv_b, f_w0, f_b0, f_w1, f_b1, f_freq, f_wout, hy_b):
    n = u.shape[1]
    u = short_conv(u, conv_w, conv_b)
    x1, x2, v = jnp.split(u, 3, axis=-1)
    filt = hyena_filters(n, f_w0, f_b0, f_w1, f_b1, f_freq, f_wout)
    z = x1 * long_conv_bidir(v, filt[:, 0, 0], filt[:, 0, 1], hy_b[0])
    z = x2 * long_conv_bidir(z, filt[:, 1, 0], filt[:, 1, 1], hy_b[1])
    return z


def mixer_ab(h_lat, h_ctx, w_in, w_out, lam_p, subln_w, conv_w, conv_b,
             f_w0, f_b0, f_w1, f_b1, f_freq, f_wout, hy_b, layer, need_ctx):
    bsz, n, _ = h_lat.shape
    n_ctx = h_ctx.shape[1]
    lam_init = 0.8 - 0.6 * math.exp(-0.3 * layer)
    lam = (jnp.exp(jnp.sum(lam_p[0] * lam_p[1]).astype(jnp.float32))
           - jnp.exp(jnp.sum(lam_p[2] * lam_p[3]).astype(jnp.float32)) + lam_init)

    def head_out(o):
        return (rmsnorm(o, subln_w) * (1.0 - lam_init)).reshape(o.shape[0], o.shape[1], A_WIDTH)

    def hy(u):
        return hyena(u, conv_w, conv_b, f_w0, f_b0, f_w1, f_b1, f_freq, f_wout, hy_b)

    u_l = h_lat @ w_in
    q_l = axial_rope(u_l[..., :A_WIDTH].reshape(bsz, n, A_HEADS, 2, A_QK_DIM))
    k_l = axial_rope(u_l[..., A_WIDTH:2 * A_WIDTH].reshape(bsz, n, A_HEADS, 2, A_QK_DIM))
    v_l = u_l[..., 2 * A_WIDTH:3 * A_WIDTH].reshape(bsz, n, A_HEADS, A_V_DIM)
    if need_ctx:
        u_c = h_ctx @ w_in
        kv_c = u_c[..., A_WIDTH:3 * A_WIDTH]
    else:
        kv_c = h_ctx @ w_in[:, A_WIDTH:3 * A_WIDTH]
    k_c = kv_c[..., :A_WIDTH].reshape(bsz, n_ctx, A_HEADS, 2, A_QK_DIM)
    v_c = kv_c[..., A_WIDTH:].reshape(bsz, n_ctx, A_HEADS, A_V_DIM)
    k_all = jnp.concatenate([k_c, k_l], axis=1)
    v_all = jnp.concatenate([v_c, v_l], axis=1)
    nb = n // QBLOCK
    q_blk = q_l.reshape(bsz, nb, QBLOCK, A_HEADS, 2, A_QK_DIM).swapaxes(0, 1)
    o_l = lax.map(lambda qb: diff_attend(qb, k_all, v_all, lam), q_blk)
    a_l = head_out(o_l.swapaxes(0, 1).reshape(bsz, n, A_HEADS, A_V_DIM))
    b_l = hy(u_l[..., 3 * A_WIDTH:])
    y_l = jnp.concatenate([a_l, b_l], axis=-1) @ w_out
    y_c = None
    if need_ctx:
        q_c = u_c[..., :A_WIDTH].reshape(bsz, n_ctx, A_HEADS, 2, A_QK_DIM)
        a_c = head_out(diff_attend(q_c, k_c, v_c, lam))
        b_c = hy(u_c[..., 3 * A_WIDTH:])
        y_c = jnp.concatenate([a_c, b_c], axis=-1) @ w_out
    return y_l, y_c


def neighbourhood_attention(q, k, v, k_ctx, v_ctx, rpb):
    bsz, n, nh, dh = q.shape
    rows = n // GRID_W
    kr = min(NA_ROWS_MAX, rows)
    kc = NA_COLS
    scale = dh ** -0.5
    qg = q.reshape(bsz, rows, GRID_W, nh, dh)
    kg = k.reshape(bsz, rows, GRID_W, nh, dh)
    vg = v.reshape(bsz, rows, GRID_W, nh, dh)
    cols = jnp.arange(GRID_W)
    col_start = jnp.clip(cols - kc // 2, 0, GRID_W - kc)
    col_idx = col_start[:, None] + jnp.arange(kc)[None, :]
    col_off = col_idx - cols[:, None] + (NA_COLS - 1)
    rpb_cols = rpb[:, :, col_off]

    def row_step(args):
        r, q_row = args
        rs = jnp.clip(r - kr // 2, 0, rows - kr)
        k_nb = lax.dynamic_slice_in_dim(kg, rs, kr, axis=1)[:, :, col_idx]
        v_nb = lax.dynamic_slice_in_dim(vg, rs, kr, axis=1)[:, :, col_idx]
        row_off = rs + jnp.arange(kr) - r + (NA_ROWS_MAX - 1)
        bias = rpb_cols[:, row_off].transpose(0, 2, 1, 3)
        s_lat = jnp.einsum('bchd,brckhd->bhcrk', q_row, k_nb).astype(jnp.float32) * scale + bias
        s_ctx = jnp.einsum('bchd,bkhd->bhck', q_row, k_ctx).astype(jnp.float32) * scale
        s = jnp.concatenate([s_lat.reshape(bsz, nh, GRID_W, kr * kc), s_ctx], axis=-1)
        p = jax.nn.softmax(s, axis=-1).astype(v.dtype)
        p_lat = p[..., :kr * kc].reshape(bsz, nh, GRID_W, kr, kc)
        return (jnp.einsum('bhcrk,brckhd->bchd', p_lat, v_nb)
                + jnp.einsum('bhck,bkhd->bchd', p[..., kr * kc:], v_ctx))

    out = lax.map(row_step, (jnp.arange(rows), qg.swapaxes(0, 1)))
    return out.swapaxes(0, 1).reshape(bsz, n, nh * dh)


def mixer_c(h_lat, h_ctx, w_in, w_out, rpb, need_ctx):
    bsz, n, _ = h_lat.shape
    n_ctx = h_ctx.shape[1]
    u_l = (h_lat @ w_in).reshape(bsz, n, 3, C_HEADS, C_HEAD_DIM)
    if need_ctx:
        u_c = (h_ctx @ w_in).reshape(bsz, n_ctx, 3, C_HEADS, C_HEAD_DIM)
        k_c, v_c = u_c[:, :, 1], u_c[:, :, 2]
    else:
        kv_c = (h_ctx @ w_in[:, C_WIDTH:]).reshape(bsz, n_ctx, 2, C_HEADS, C_HEAD_DIM)
        k_c, v_c = kv_c[:, :, 0], kv_c[:, :, 1]
    o_l = neighbourhood_attention(u_l[:, :, 0], u_l[:, :, 1], u_l[:, :, 2], k_c, v_c, rpb)
    y_l = o_l @ w_out
    y_c = None
    if need_ctx:
        y_c = plain_attend(u_c[:, :, 0], k_c, v_c).reshape(bsz, n_ctx, C_WIDTH) @ w_out
    return y_l, y_c


def setup_inputs(seed: int = 0) -> dict:
    key = jax.random.key(seed)
    ks = jax.random.split(key, 27)
    f32 = jnp.float32
    D = D_MODEL

    def nrm(k, shape, scale):
        return jax.random.normal(k, shape, f32) * scale

    return {
        'x': nrm(ks[0], (BATCH, SEQ, D), 1.0),
        'c': nrm(ks[1], (BATCH, D), 1.0),
        'ctx': nrm(ks[2], (BATCH, CTX_LEN, D), 1.0),
        'c_ctx': nrm(ks[3], (D,), 1.0),
        'mod_w': nrm(ks[4], (DEPTH, D, N_MOD * D), 0.5 * D ** -0.5),
        'mod_b': nrm(ks[5], (DEPTH, N_MOD * D), 0.02),
        'norm_w': 1.0 + nrm(ks[6], (DEPTH, 3, D), 0.02),
        'ffn_w1': nrm(ks[7], (DEPTH, 2, D, D_FF), D ** -0.5),
        'ffn_w3': nrm(ks[8], (DEPTH, 2, D, D_FF), D ** -0.5),
        'ffn_w2': nrm(ks[9], (DEPTH, 2, D_FF, D), D_FF ** -0.5),
        'ab_w_in': nrm(ks[10], (N_EVEN, D, AB_IN), D ** -0.5),
        'ab_w_out': nrm(ks[11], (N_EVEN, MIX_WIDTH, D), MIX_WIDTH ** -0.5),
        'diff_lambda': nrm(ks[12], (N_EVEN, 4, A_QK_DIM), 0.1),
        'diff_subln_w': 1.0 + nrm(ks[13], (N_EVEN, A_V_DIM), 0.02),
        'hy_conv_w': nrm(ks[14], (N_EVEN, HY_SHORT, HY_IN), HY_SHORT ** -0.5),
        'hy_conv_b': nrm(ks[15], (N_EVEN, HY_IN), 0.02),
        'hy_f_w0': nrm(ks[16], (N_EVEN, HY_EMB, HY_FILTER_HID), HY_EMB ** -0.5),
        'hy_f_b0': nrm(ks[17], (N_EVEN, HY_FILTER_HID), 0.1),
        'hy_f_w1': nrm(ks[18], (N_EVEN, HY_INNER, HY_FILTER_HID, HY_FILTER_HID), HY_FILTER_HID ** -0.5),
        'hy_f_b1': nrm(ks[19], (N_EVEN, HY_INNER, HY_FILTER_HID), 0.1),
        'hy_f_freq': 1.0 + nrm(ks[20], (N_EVEN, HY_FILTER_HID), 0.1),
        'hy_f_wout': nrm(ks[21], (N_EVEN, HY_FILTER_HID, HY_ORDER * 2 * B_WIDTH), HY_FILTER_STD * HY_FILTER_HID ** -0.5),
        'hy_bias': nrm(ks[22], (N_EVEN, HY_ORDER, B_WIDTH), 0.5),
        'na_w_in': nrm(ks[23], (N_ODD, D, 3 * C_WIDTH), D ** -0.5),
        'na_w_out': nrm(ks[24], (N_ODD, C_WIDTH, D), C_WIDTH ** -0.5),
        'na_rpb': nrm(ks[25], (N_ODD, C_HEADS, 2 * NA_ROWS_MAX - 1, 2 * NA_COLS - 1), 0.02),
        'final_norm_w': 1.0 + nrm(ks[26], (D,), 0.02),
    }


def reference(x, c, ctx, c_ctx, mod_w, mod_b, norm_w, ffn_w1, ffn_w3, ffn_w2,
              ab_w_in, ab_w_out, diff_lambda, diff_subln_w, hy_conv_w, hy_conv_b,
              hy_f_w0, hy_f_b0, hy_f_w1, hy_f_b1, hy_f_freq, hy_f_wout, hy_bias,
              na_w_in, na_w_out, na_rpb, final_norm_w):
    bsz = x.shape[0]
    lat, cx = x, ctx
    s_l = jax.nn.silu(c)
    s_c = jax.nn.silu(c_ctx)[None]
    for layer in range(DEPTH):
        need_ctx = layer < DEPTH - 1
        mod_l = (s_l @ mod_w[layer] + mod_b[layer]).reshape(bsz, N_MOD, 1, D_MODEL)
        mod_c = (s_c @ mod_w[layer] + mod_b[layer]).reshape(1, N_MOD, 1, D_MODEL)
        g = norm_w[layer]
        f0 = (ffn_w1[layer, 0], ffn_w3[layer, 0], ffn_w2[layer, 0])
        f1 = (ffn_w1[layer, 1], ffn_w3[layer, 1], ffn_w2[layer, 1])
        lat = macaron(lat, g[0], mod_l, 0, *f0)
        cx = macaron(cx, g[0], mod_c, 0, *f0)
        h_l = adaln_in(lat, g[1], mod_l, 1)
        h_c = adaln_in(cx, g[1], mod_c, 1)
        if layer % 2 == 0:
            i = layer // 2
            y_l, y_c = mixer_ab(h_l, h_c, ab_w_in[i], ab_w_out[i], diff_lambda[i], diff_subln_w[i],
                                hy_conv_w[i], hy_conv_b[i], hy_f_w0[i], hy_f_b0[i], hy_f_w1[i],
                                hy_f_b1[i], hy_f_freq[i], hy_f_wout[i], hy_bias[i], layer, need_ctx)
        else:
            i = layer // 2
            y_l, y_c = mixer_c(h_l, h_c, na_w_in[i], na_w_out[i], na_rpb[i], need_ctx)
        lat = macaron(lat + mod_l[:, 5] * y_l, g[2], mod_l, 2, *f1)
        if need_ctx:
            cx = macaron(cx + mod_c[:, 5] * y_c, g[2], mod_c, 2, *f1)
    return rmsnorm(lat, final_norm_w)
```

```python
import functools
import math

import jax
import jax.numpy as jnp
from jax import lax
from jax.experimental import pallas as pl
from jax.experimental.pallas import tpu as pltpu

F32 = jnp.float32
BF16 = jnp.bfloat16
HIGHEST = lax.Precision.HIGHEST

GRID_W = 64
N_MOD = 9
NORM_EPS = 1e-6
A_HEADS = 4
A_QK_DIM = 64
A_V_DIM = 128
HY_EMB = 33
HY_BANDS = 16
HY_TARGET = 1e-2
HY_MAX_DECAY = math.log(HY_TARGET) / 0.3
HY_MIN_DECAY = math.log(HY_TARGET) / 1.5
C_HEADS = 16
C_HEAD_DIM = 64
NA_ROWS = 8
NA_COLS = 16
ROPE_BASE = 10000.0

LANES = 128
SUBLANES = 8
VMEM_LIMIT = 56 * 2**20

NA_QROWS = 4
NA_KROWS = NA_QROWS + NA_ROWS
NEG = -1e30

DFT_L = 128
P1 = DFT_L + 8


def _cparams(sem):
    return pltpu.CompilerParams(dimension_semantics=sem, vmem_limit_bytes=VMEM_LIMIT)


def _const_spec(shape):
    nd = len(shape)
    return pl.BlockSpec(shape, lambda *_: (0,) * nd, pipeline_mode=pl.Buffered(1))


def _rms(x, w):
    ms = jnp.mean(x * x, axis=-1, keepdims=True)
    return x * lax.rsqrt(ms + NORM_EPS) * w


def _sigmoid(x):
    return 1.0 / (1.0 + jnp.exp(-x))


def _mod_kernel(s_ref, w_ref, b_ref, o_ref):
    s = s_ref[...]
    s = s * _sigmoid(s)
    o_ref[0] = jnp.dot(s, w_ref[0], precision=HIGHEST, preferred_element_type=F32) + b_ref[0]


def _modulation(c, c_ctx, mod_w, mod_b):
    depth, d, nd = mod_w.shape
    bsz = c.shape[0]
    rows = -(-(bsz + 1) // SUBLANES) * SUBLANES
    s = jnp.zeros((rows, d), F32).at[:bsz].set(c).at[bsz].set(c_ctx)
    tn = 1024
    out = pl.pallas_call(
        _mod_kernel,
        grid=(depth, nd // tn),
        in_specs=[
            pl.BlockSpec((rows, d), lambda l, j: (0, 0)),
            pl.BlockSpec((1, d, tn), lambda l, j: (l, 0, j)),
            pl.BlockSpec((1, 1, tn), lambda l, j: (l, 0, j)),
        ],
        out_specs=pl.BlockSpec((1, rows, tn), lambda l, j: (l, 0, j)),
        out_shape=jax.ShapeDtypeStruct((depth, rows, nd), F32),
        compiler_params=_cparams(("arbitrary", "arbitrary")),
    )(s, mod_w, mod_b.reshape(depth, 1, nd))
    return out[:, :bsz + 1].reshape(depth, bsz + 1, N_MOD, d)


def _ffn_kernel(*refs, k, pre_widths, final):
    it = iter(refs)
    x_ref, mod_ref, g_ref, w1_ref, w3_ref, w2_ref = (next(it) for _ in range(6))
    y_refs = [next(it) for _ in pre_widths]
    wo_ref = next(it) if pre_widths else None
    fw_ref = next(it) if final else None
    o_ref = next(it)

    x = x_ref[0]
    if pre_widths:
        off = 0
        y = None
        for y_ref, w in zip(y_refs, pre_widths):
            t = jnp.dot(y_ref[0], wo_ref[off:off + w, :], preferred_element_type=F32)
            y = t if y is None else y + t
            off += w
        x = x + mod_ref[0, 5:6, :] * y
    shift = mod_ref[0, 3 * k:3 * k + 1, :]
    scale = mod_ref[0, 3 * k + 1:3 * k + 2, :]
    gate = mod_ref[0, 3 * k + 2:3 * k + 3, :]
    hn = (_rms(x, g_ref[k:k + 1, :]) * (1.0 + scale) + shift).astype(BF16)
    a = jnp.dot(hn, w1_ref[...], preferred_element_type=F32)
    b = jnp.dot(hn, w3_ref[...], preferred_element_type=F32)
    act = (a * _sigmoid(a) * b).astype(BF16)
    out = x + 0.5 * gate * jnp.dot(act, w2_ref[...], preferred_element_type=F32)
    if final:
        out = _rms(out, fw_ref[...])
    o_ref[0] = out


def _ffn(x, mod, mod_row, g, w1, w3, w2, k, pre=None, final_w=None):
    bsz, n, d = x.shape
    d_ff = w1.shape[1]
    tm = min(n, 512)
    row = lambda b, i: (b, i, 0)
    in_specs = [
        pl.BlockSpec((1, tm, d), row),
        pl.BlockSpec((1, N_MOD, d), lambda b, i: (mod_row(b), 0, 0)),
        _const_spec(g.shape),
        _const_spec((d, d_ff)),
        _const_spec((d, d_ff)),
        _const_spec((d_ff, d)),
    ]
    args = [x, mod, g, w1, w3, w2]
    pre_widths = ()
    if pre is not None:
        ys, wo = pre
        pre_widths = tuple(y.shape[-1] for y in ys)
        for y in ys:
            in_specs.append(pl.BlockSpec((1, tm, y.shape[-1]), row))
            args.append(y)
        in_specs.append(_const_spec(wo.shape))
        args.append(wo)
    if final_w is not None:
        in_specs.append(_const_spec((1, d)))
        args.append(final_w.reshape(1, d))
    return pl.pallas_call(
        functools.partial(_ffn_kernel, k=k, pre_widths=pre_widths, final=final_w is not None),
        grid=(bsz, n // tm),
        in_specs=in_specs,
        out_specs=pl.BlockSpec((1, tm, d), row),
        out_shape=jax.ShapeDtypeStruct((bsz, n, d), F32),
        compiler_params=_cparams(("parallel", "parallel")),
    )(*args)


def _rope(u, cos, sin_signed):
    w = u.shape[-1]
    lane = lax.broadcasted_iota(jnp.int32, u.shape, 1)
    partner = jnp.where(lane % 32 < 16, pltpu.roll(u, w - 16, 1), pltpu.roll(u, 16, 1))
    return u * cos + partner * sin_signed


def _proj_kernel(*refs, segments, use_rope):
    it = iter(refs)
    x_ref, mod_ref, g_ref, w_ref = (next(it) for _ in range(4))
    cos_ref = next(it) if use_rope else None
    sin_ref = next(it) if use_rope else None
    out_refs = list(it)
    x = x_ref[0]
    hn = (_rms(x, g_ref[1:2, :]) * (1.0 + mod_ref[0, 4:5, :]) + mod_ref[0, 3:4, :]).astype(BF16)
    off = 0
    for o_ref, (width, scale, rope) in zip(out_refs, segments):
        u = jnp.dot(hn, w_ref[:, off:off + width], preferred_element_type=F32)
        if rope:
            u = _rope(u, cos_ref[...], sin_ref[...])
        if scale != 1.0:
            u = u * scale
        o_ref[0] = u.astype(o_ref.dtype)
        off += width


def _project(x, mod, mod_row, g, w, segments, rope_tables=None):
    bsz, n, d = x.shape
    tm = min(n, 512)
    row = lambda b, i: (b, i, 0)
    in_specs = [
        pl.BlockSpec((1, tm, d), row),
        pl.BlockSpec((1, N_MOD, d), lambda b, i: (mod_row(b), 0, 0)),
        _const_spec(g.shape),
        _const_spec(w.shape),
    ]
    args = [x, mod, g, w]
    if rope_tables is not None:
        for t in rope_tables:
            in_specs.append(pl.BlockSpec((tm, t.shape[1]), lambda b, i: (i, 0)))
            args.append(t)
    return pl.pallas_call(
        functools.partial(_proj_kernel, segments=tuple(segments), use_rope=rope_tables is not None),
        grid=(bsz, n // tm),
        in_specs=in_specs,
        out_specs=[pl.BlockSpec((1, tm, s[0]), row) for s in segments],
        out_shape=[jax.ShapeDtypeStruct((bsz, n, s[0]), BF16) for s in segments],
        compiler_params=_cparams(("parallel", "parallel")),
    )(*args)


def _rope_tables(n, reps):
    t = jnp.arange(n)
    nf = A_QK_DIM // 4
    inv_freq = ROPE_BASE ** (-jnp.arange(nf, dtype=F32) / nf)
    cols, sins = [], []
    for pos in (t // GRID_W, t % GRID_W):
        ang = pos.astype(F32)[:, None] * inv_freq
        c, s = jnp.cos(ang), jnp.sin(ang)
        cols += [c, c]
        sins += [-s, s]
    return jnp.tile(jnp.concatenate(cols, 1), (1, reps)), jnp.tile(jnp.concatenate(sins, 1), (1, reps))


def _softmax_pv(qm, k_refs, v_refs):
    nt = (((1,), (1,)), ((), ()))
    ss = [lax.dot_general(qm, k_ref[0], nt, preferred_element_type=F32) for k_ref in k_refs]
    m = None
    for s in ss:
        ms = jnp.max(s, axis=-1, keepdims=True)
        m = ms if m is None else jnp.maximum(m, ms)
    denom = None
    o = None
    for s, v_ref in zip(ss, v_refs):
        e = jnp.exp(s - m)
        ds = jnp.sum(e, axis=-1, keepdims=True)
        os_ = jnp.dot(e.astype(BF16), v_ref[0], preferred_element_type=F32)
        denom = ds if denom is None else denom + ds
        o = os_ if o is None else o + os_
    return o / denom


def _dattn_kernel(*refs, n_src, lam_init):
    lam_ref, sw_ref, q_ref = refs[:3]
    k_refs = refs[3:3 + n_src]
    v_refs = refs[3 + n_src:3 + 2 * n_src]
    o_ref = refs[3 + 2 * n_src]
    q = q_ref[0]
    lane = lax.broadcasted_iota(jnp.int32, q.shape, 1)
    zero = jnp.zeros_like(q)
    o0 = _softmax_pv(jnp.where(lane < A_QK_DIM, q, zero), k_refs, v_refs)
    o1 = _softmax_pv(jnp.where(lane >= A_QK_DIM, q, zero), k_refs, v_refs)
    lp = lam_ref[...]
    lam = (jnp.exp(jnp.sum(lp[0:1] * lp[1:2], axis=-1, keepdims=True))
           - jnp.exp(jnp.sum(lp[2:3] * lp[3:4], axis=-1, keepdims=True)) + lam_init)
    o = o0 - lam * o1
    o_ref[0] = (_rms(o, sw_ref[...]) * (1.0 - lam_init)).astype(o_ref.dtype)


def _diff_attention(q, ks, vs, lam_p, subln_w, lam_init):
    bsz, nq, width = q.shape
    tq = min(nq, 256)
    hb = lambda b, h, i: (b, i, h)
    kv = lambda b, h, i: (b, 0, h)
    in_specs = [
        pl.BlockSpec(lam_p.shape, lambda b, h, i: (0, 0)),
        pl.BlockSpec((1, A_V_DIM), lambda b, h, i: (0, 0)),
        pl.BlockSpec((1, tq, LANES), hb),
    ]
    in_specs += [pl.BlockSpec((1, k.shape[1], LANES), kv) for k in ks]
    in_specs += [pl.BlockSpec((1, v.shape[1], LANES), kv) for v in vs]
    return pl.pallas_call(
        functools.partial(_dattn_kernel, n_src=len(ks), lam_init=lam_init),
        grid=(bsz, A_HEADS, nq // tq),
        in_specs=in_specs,
        out_specs=pl.BlockSpec((1, tq, LANES), hb),
        out_shape=jax.ShapeDtypeStruct((bsz, nq, width), BF16),
        compiler_params=_cparams(("parallel", "parallel", "parallel")),
    )(lam_p, subln_w.reshape(1, A_V_DIM), q, *ks, *vs)


def _na_kernel(q_ref, k_ref, v_ref, kc_ref, vc_ref, tb_ref, o_ref, *, rows):
    i = pl.program_id(2)
    r0 = i * NA_QROWS
    start = jnp.clip(r0 - NA_ROWS // 2, 0, rows - NA_KROWS)
    koff = pl.multiple_of(start * GRID_W, GRID_W)
    kw = k_ref[0, pl.ds(koff, NA_KROWS * GRID_W), :]
    vw = v_ref[0, pl.ds(koff, NA_KROWS * GRID_W), :]
    kc = kc_ref[0]
    vc = vc_ref[0]
    q = q_ref[0]
    lane = lax.broadcasted_iota(jnp.int32, q.shape, 1)
    zero = jnp.zeros_like(q)
    lane_b = lax.broadcasted_iota(jnp.int32, (GRID_W, LANES), 1)
    nt = (((1,), (1,)), ((), ()))
    outs = []
    for hh in range(2):
        in_head = (lane < C_HEAD_DIM) if hh == 0 else (lane >= C_HEAD_DIM)
        qm = jnp.where(in_head, q, zero)
        s_lat = lax.dot_general(qm, kw, nt, preferred_element_type=F32)
        s_ctx = lax.dot_general(qm, kc, nt, preferred_element_type=F32)
        bias_rows = []
        for j in range(NA_QROWS):
            r = r0 + j
            rs = jnp.clip(r - NA_ROWS // 2, 0, rows - NA_ROWS)
            blocks = []
            for mp in range(NA_KROWS // 2):
                idx = []
                for m in (2 * mp, 2 * mp + 1):
                    kr = start + m
                    valid = jnp.logical_and(kr >= rs, kr < rs + NA_ROWS)
                    idx.append(jnp.where(valid, kr - r + NA_ROWS - 1, 2 * NA_ROWS - 1))
                blocks.append(jnp.where(lane_b < GRID_W, tb_ref[hh, idx[0]], tb_ref[hh, idx[1]]))
            bias_rows.append(jnp.concatenate(blocks, axis=1))
        s_lat = s_lat + jnp.concatenate(bias_rows, axis=0)
        m = jnp.maximum(jnp.max(s_lat, axis=-1, keepdims=True), jnp.max(s_ctx, axis=-1, keepdims=True))
        e_lat = jnp.exp(s_lat - m)
        e_ctx = jnp.exp(s_ctx - m)
        denom = jnp.sum(e_lat, axis=-1, keepdims=True) + jnp.sum(e_ctx, axis=-1, keepdims=True)
        o = (jnp.dot(e_lat.astype(BF16), vw, preferred_element_type=F32)
             + jnp.dot(e_ctx.astype(BF16), vc, preferred_element_type=F32))
        outs.append(o / denom)
    o_ref[0] = jnp.where(lane < C_HEAD_DIM, outs[0], outs[1]).astype(o_ref.dtype)


def _na_bias_table(rpb):
    nh = rpb.shape[0]
    c = jnp.arange(GRID_W)
    cs = jnp.clip(c - NA_COLS // 2, 0, GRID_W - NA_COLS)
    kc = jnp.arange(GRID_W)
    inside = (kc[None, :] >= cs[:, None]) & (kc[None, :] < cs[:, None] + NA_COLS)
    off = jnp.clip(kc[None, :] - c[:, None] + NA_COLS - 1, 0, 2 * NA_COLS - 2)
    tb = jnp.where(inside[None, None], rpb[:, :, off], NEG)
    tb = jnp.concatenate([tb, jnp.full((nh, 1, GRID_W, GRID_W), NEG, F32)], axis=1)
    return jnp.concatenate([tb, tb], axis=-1)


def _neighbourhood_attention(q, k, v, k_ctx, v_ctx, rpb):
    bsz, n, width = q.shape
    rows = n // GRID_W
    assert rows >= NA_KROWS and rows % NA_QROWS == 0
    n_ctx = k_ctx.shape[1]
    tq = NA_QROWS * GRID_W
    tb = _na_bias_table(rpb)
    kv = lambda b, h, i: (b, 0, h)
    return pl.pallas_call(
        functools.partial(_na_kernel, rows=rows),
        grid=(bsz, C_HEADS // 2, n // tq),
        in_specs=[
            pl.BlockSpec((1, tq, LANES), lambda b, h, i: (b, i, h)),
            pl.BlockSpec((1, n, LANES), kv),
            pl.BlockSpec((1, n, LANES), kv),
            pl.BlockSpec((1, n_ctx, LANES), kv),
            pl.BlockSpec((1, n_ctx, LANES), kv),
            pl.BlockSpec((2, 2 * NA_ROWS, GRID_W, LANES), lambda b, h, i: (h, 0, 0, 0)),
        ],
        out_specs=pl.BlockSpec((1, tq, LANES), lambda b, h, i: (b, i, h)),
        out_shape=jax.ShapeDtypeStruct((bsz, n, width), BF16),
        compiler_params=_cparams(("parallel", "parallel", "arbitrary")),
    )(q, k, v, k_ctx, v_ctx, tb)


def _filter_kernel(z_ref, w0_ref, b0_ref, w1_ref, b1_ref, fr_ref, wo_ref, dl_ref, o_ref, *, n_inner, cw):
    z = z_ref[...]
    fr = fr_ref[...]
    hid = jnp.sin(fr * (jnp.dot(z, w0_ref[...], precision=HIGHEST, preferred_element_type=F32) + b0_ref[...]))
    for i in range(n_inner):
        hid = jnp.sin(fr * (jnp.dot(hid, w1_ref[i], precision=HIGHEST, preferred_element_type=F32)
                            + b1_ref[i:i + 1, :]))
    filt = jnp.dot(hid, wo_ref[...], precision=HIGHEST, preferred_element_type=F32)
    window = jnp.exp(-z[:, 0:1] * dl_ref[...])
    for j in range(filt.shape[1] // cw):
        o_ref[:, j * cw:(j + 1) * cw] = filt[:, j * cw:(j + 1) * cw] * window


def _hyena_filters(n, f_w0, f_b0, f_w1, f_b1, f_freq, f_wout, cw):
    t = jnp.linspace(0.0, 1.0, n, dtype=F32)[:, None]
    w = (2.0 * math.pi / n) * jnp.arange(n, dtype=F32)[:, None]
    bands = jnp.linspace(1e-4, HY_BANDS - 1, HY_BANDS, dtype=F32)[None, :]
    z = jnp.concatenate([t, jnp.cos(bands * w), -jnp.sin(bands * w)], axis=-1)
    z = jnp.pad(z, ((0, 0), (0, LANES - HY_EMB)))
    w0 = jnp.pad(f_w0, ((0, LANES - HY_EMB), (0, 0)))
    deltas = jnp.abs(jnp.linspace(HY_MIN_DECAY, HY_MAX_DECAY, cw, dtype=F32))[None, :]
    hid = f_w0.shape[1]
    tt = min(n, 512)
    full = lambda shape: pl.BlockSpec(shape, lambda i: (0,) * len(shape))
    return pl.pallas_call(
        functools.partial(_filter_kernel, n_inner=f_w1.shape[0], cw=cw),
        grid=(n // tt,),
        in_specs=[
            pl.BlockSpec((tt, LANES), lambda i: (i, 0)),
            full(w0.shape), full((1, hid)), full(f_w1.shape), full(f_b1.shape), full((1, hid)),
            full(f_wout.shape), full((1, cw)),
        ],
        out_specs=pl.BlockSpec((tt, f_wout.shape[1]), lambda i: (i, 0)),
        out_shape=jax.ShapeDtypeStruct((n, f_wout.shape[1]), F32),
        compiler_params=_cparams(("parallel",)),
    )(z, w0, f_b0.reshape(1, hid), f_w1, f_b1, f_freq.reshape(1, hid), f_wout, deltas)


def _filter_taps(filt, cw):
    n = filt.shape[0]
    firsts, seconds = [], []
    for o in range(filt.shape[1] // (2 * cw)):
        h_fwd = filt[:, 2 * o * cw:(2 * o + 1) * cw]
        h_bwd = filt[:, (2 * o + 1) * cw:(2 * o + 2) * cw]
        firsts.append(h_fwd)
        seconds.append(jnp.concatenate([jnp.zeros((1, cw), F32), h_bwd[:0:-1]], axis=0))
    return jnp.concatenate(firsts, 1), jnp.concatenate(seconds, 1)


def _cis(phase, period, sign):
    ang = (2.0 * math.pi / period) * (phase % period).astype(F32)
    return jnp.cos(ang), sign * jnp.sin(ang)


def _cblock(mr, mi):
    return jnp.concatenate([jnp.concatenate([mr, -mi], -1), jnp.concatenate([mi, mr], -1)], -2)


def _dft_tables(n):
    big = 2 * n
    nb = big // DFT_L
    f = jnp.arange(DFT_L, dtype=jnp.int32)
    k1 = jnp.arange(nb, dtype=jnp.int32)
    blk = jnp.arange(nb, dtype=jnp.int32)
    ph1 = f[:, None, None] * k1[None, :, None] + DFT_L * blk[None, None, :] * k1[None, :, None]
    cr, ci = _cis(ph1, big, -1.0)
    t1_data = _cblock(cr[:, :, :nb // 2], ci[:, :, :nb // 2])
    t1_real = jnp.concatenate([cr, ci], axis=1)
    ph2 = f[:, None] * f[None, :]
    fr, fi = _cis(ph2, DFT_L, -1.0)
    t2 = _cblock(fr, fi)
    t2_inv = _cblock(fr, -fi)
    n1 = jnp.arange(nb // 2, dtype=jnp.int32)
    ph3 = DFT_L * n1[None, :, None] * k1[None, None, :] + f[:, None, None] * k1[None, None, :]
    dr, di = _cis(ph3, big, 1.0)
    t3 = _cblock(dr, di) / big
    return t1_data.astype(BF16), t1_real.astype(BF16), t2.astype(BF16), t2_inv.astype(BF16), t3.astype(BF16)


def _dft_stage1(xa_ref, xb_ref, t1_ref, a_ref, nb):
    half = nb // 2

    def body(f, carry):
        rhs = jnp.concatenate([xa_ref[pl.ds(f, half, stride=DFT_L), :],
                               xb_ref[pl.ds(f, half, stride=DFT_L), :]], axis=0).astype(BF16)
        out = jnp.dot(t1_ref[f], rhs, preferred_element_type=F32)
        a_ref[pl.ds(pl.multiple_of(f * P1, SUBLANES), 2 * nb), :] = out
        return carry

    lax.fori_loop(0, DFT_L, body, 0)


def _dft_stage2_in(a_ref, k1, nb):
    return jnp.concatenate([a_ref[pl.ds(k1, DFT_L, stride=P1), :],
                            a_ref[pl.ds(nb + k1, DFT_L, stride=P1), :]], axis=0).astype(BF16)


def _spectrum_kernel(xa_ref, xb_ref, t1_ref, t2_ref, o_ref, a_ref, *, nb):
    _dft_stage1(xa_ref, xb_ref, t1_ref, a_ref, nb)

    def body(k1, carry):
        out = jnp.dot(t2_ref[...], _dft_stage2_in(a_ref, k1, nb), preferred_element_type=F32)
        o_ref[pl.ds(pl.multiple_of(k1 * 2 * DFT_L, 2 * DFT_L), 2 * DFT_L), :] = out
        return carry

    lax.fori_loop(0, nb, body, 0)


def _filter_spectrum(first, second, t1_real, t2):
    n, cols = first.shape
    nb = 2 * n // DFT_L
    col = lambda j: (0, j)
    return pl.pallas_call(
        functools.partial(_spectrum_kernel, nb=nb),
        grid=(cols // LANES,),
        in_specs=[
            pl.BlockSpec((n, LANES), col),
            pl.BlockSpec((n, LANES), col),
            _const_spec(t1_real.shape),
            _const_spec(t2.shape),
        ],
        out_specs=pl.BlockSpec((4 * n, LANES), col),
        out_shape=jax.ShapeDtypeStruct((4 * n, cols), F32),
        scratch_shapes=[pltpu.VMEM((DFT_L * P1, LANES), F32)],
        compiler_params=_cparams(("parallel",)),
    )(first, second, t1_real, t2)


def _short_conv_kernel(u_ref, w_ref, b_ref, o_ref):
    u = u_ref[0].astype(F32)
    n = u.shape[0]
    row = lax.broadcasted_iota(jnp.int32, u.shape, 0)
    prev = jnp.where(row == 0, 0.0, pltpu.roll(u, 1, 0))
    nxt = jnp.where(row == n - 1, 0.0, pltpu.roll(u, n - 1, 0))
    out = prev * w_ref[0:1, :] + u * w_ref[1:2, :] + nxt * w_ref[2:3, :] + b_ref[...]
    o_ref[0] = out.astype(o_ref.dtype)


def _short_conv(u, u_blk, w, b):
    bsz, n, _ = u.shape
    taps, ch = w.shape
    return pl.pallas_call(
        _short_conv_kernel,
        grid=(bsz, ch // LANES),
        in_specs=[
            pl.BlockSpec((1, n, LANES), lambda bi, j: (bi, 0, u_blk + j)),
            pl.BlockSpec((taps, LANES), lambda bi, j: (0, j)),
            pl.BlockSpec((1, LANES), lambda bi, j: (0, j)),
        ],
        out_specs=pl.BlockSpec((1, n, LANES), lambda bi, j: (bi, 0, j)),
        out_shape=jax.ShapeDtypeStruct((bsz, n, ch), BF16),
        compiler_params=_cparams(("parallel", "parallel")),
    )(u, w, b.reshape(1, ch))


def _cmul_store(s, g, rows):
    sr, si = s[:rows], s[rows:]
    gr, gi = g[:rows], g[rows:]
    return jnp.concatenate([sr * gr - si * gi, sr * gi + si * gr], 0).astype(BF16)


def _hyena_conv_kernel(gate_ref, v_ref, skip_ref, g_ref, t1_ref, t2_ref, t2i_ref, t3_ref, o_ref,
                       va_ref, vb_ref, a_ref, p_ref, y_ref, *, nb, p2, p3):
    half = nb // 2
    two_l = 2 * DFT_L
    va_ref[...] = v_ref[0].astype(F32)
    vb_ref[...] = v_ref[1].astype(F32)

    _dft_stage1(va_ref, vb_ref, t1_ref, a_ref, nb)

    def fwd2(k1, carry):
        s = jnp.dot(t2_ref[...], _dft_stage2_in(a_ref, k1, nb), preferred_element_type=F32)
        off = pl.multiple_of(k1 * two_l, two_l)
        p_ref[pl.ds(off, two_l), :] = _cmul_store(s, g_ref[pl.ds(off, two_l), :], DFT_L)
        return carry

    lax.fori_loop(0, nb, fwd2, 0)

    def inv1(k1, carry):
        off = pl.multiple_of(k1 * two_l, two_l)
        out = jnp.dot(t2i_ref[...], p_ref[pl.ds(off, two_l), :], preferred_element_type=F32)
        a_ref[pl.ds(pl.multiple_of(k1 * p2, SUBLANES), two_l), :] = out
        return carry

    lax.fori_loop(0, nb, inv1, 0)

    def inv2(n2, carry):
        rhs = jnp.concatenate([a_ref[pl.ds(n2, nb, stride=p2), :],
                               a_ref[pl.ds(DFT_L + n2, nb, stride=p2), :]], axis=0).astype(BF16)
        out = jnp.dot(t3_ref[n2], rhs, preferred_element_type=F32)
        y_ref[pl.ds(pl.multiple_of(n2 * p3, SUBLANES), nb), :] = out
        return carry

    lax.fori_loop(0, DFT_L, inv2, 0)

    skip = skip_ref[...]

    def epilogue(n1, carry):
        off = pl.multiple_of(n1 * DFT_L, DFT_L)
        for b, vs_ref in enumerate((va_ref, vb_ref)):
            y = y_ref[pl.ds(b * half + n1, DFT_L, stride=p3), :]
            gate = gate_ref[b, pl.ds(off, DFT_L), :].astype(F32)
            o_ref[b, pl.ds(off, DFT_L), :] = (gate * (y + skip * vs_ref[pl.ds(off, DFT_L), :])).astype(o_ref.dtype)
        return carry

    lax.fori_loop(0, half, epilogue, 0)


def _pitch(rows):
    p = -(-rows // SUBLANES)
    if p % 2 == 0:
        p += 1
    return p * SUBLANES


def _conv_specs(n, gate_blk, v_blk, spec_rows, spec_blk):
    return [
        pl.BlockSpec((2, n, LANES), lambda j, p: (p, 0, gate_blk + j)),
        pl.BlockSpec((2, n, LANES), lambda j, p: (p, 0, v_blk + j)),
        pl.BlockSpec((1, LANES), lambda j, p: (0, j)),
        pl.BlockSpec((spec_rows, LANES), lambda j, p: (0, spec_blk + j), pipeline_mode=pl.Buffered(1)),
    ]


def _hyena_conv(gate, gate_blk, v, v_blk, skip, spec, spec_blk, tables):
    bsz, n, _ = v.shape
    cw = skip.shape[-1]
    out_shape = jax.ShapeDtypeStruct((bsz, n, cw), BF16)
    out_spec = pl.BlockSpec((2, n, LANES), lambda j, p: (p, 0, j))
    grid = (cw // LANES, bsz // 2)
    params = _cparams(("parallel", "arbitrary"))
    if len(tables) == 2:
        fwd, inv = tables
        return pl.pallas_call(
            _hyena_conv_small_kernel,
            grid=grid,
            in_specs=_conv_specs(n, gate_blk, v_blk, 4 * n, spec_blk) + [_const_spec(fwd.shape), _const_spec(inv.shape)],
            out_specs=out_spec,
            out_shape=out_shape,
            compiler_params=params,
        )(gate, v, skip, spec, fwd, inv)
    nb = 2 * n // DFT_L
    t1, t2, t2i, t3 = tables
    p2, p3 = _pitch(2 * DFT_L), _pitch(nb)
    return pl.pallas_call(
        functools.partial(_hyena_conv_kernel, nb=nb, p2=p2, p3=p3),
        grid=grid,
        in_specs=_conv_specs(n, gate_blk, v_blk, 4 * n, spec_blk) + [_const_spec(t.shape) for t in tables],
        out_specs=out_spec,
        out_shape=out_shape,
        scratch_shapes=[
            pltpu.VMEM((n, LANES), F32), pltpu.VMEM((n, LANES), F32),
            pltpu.VMEM((max(DFT_L * P1, nb * p2), LANES), F32),
            pltpu.VMEM((4 * n, LANES), BF16),
            pltpu.VMEM((DFT_L * p3, LANES), F32),
        ],
        compiler_params=params,
    )(gate, v, skip, spec, t1, t2, t2i, t3)


def _small_tables(n):
    big = 2 * n
    k = jnp.arange(big, dtype=jnp.int32)
    fr, fi = _cis(k[:, None] * k[None, :], big, -1.0)
    fwd = _cblock(fr[:, :n], fi[:, :n])
    real = jnp.concatenate([fr, fi], axis=0)
    inv = _cblock(fr[:n, :], -fi[:n, :]) / big
    return fwd.astype(BF16), real.astype(BF16), inv.astype(BF16)


def _spectrum_small_kernel(xa_ref, xb_ref, f_ref, o_ref):
    taps = jnp.concatenate([xa_ref[...], xb_ref[...]], axis=0).astype(BF16)
    o_ref[...] = jnp.dot(f_ref[...], taps, preferred_element_type=F32)


def _filter_spectrum_small(first, second, real):
    n, cols = first.shape
    col = lambda j: (0, j)
    return pl.pallas_call(
        _spectrum_small_kernel,
        grid=(cols // LANES,),
        in_specs=[pl.BlockSpec((n, LANES), col), pl.BlockSpec((n, LANES), col), _const_spec(real.shape)],
        out_specs=pl.BlockSpec((4 * n, LANES), col),
        out_shape=jax.ShapeDtypeStruct((4 * n, cols), F32),
        compiler_params=_cparams(("parallel",)),
    )(first, second, real)


def _hyena_conv_small_kernel(gate_ref, v_ref, skip_ref, g_ref, f_ref, i_ref, o_ref):
    n = v_ref.shape[1]
    rhs = jnp.concatenate([v_ref[0], v_ref[1]], axis=0)
    s = jnp.dot(f_ref[...], rhs, preferred_element_type=F32)
    y = jnp.dot(i_ref[...], _cmul_store(s, g_ref[...], 2 * n), preferred_element_type=F32)
    skip = skip_ref[...]
    for b in range(2):
        yb = y[b * n:(b + 1) * n] + skip * v_ref[b].astype(F32)
        o_ref[b] = (gate_ref[b].astype(F32) * yb).astype(o_ref.dtype)


def _hyena(u, u_blk, conv_w, conv_b, hy_b, spec, tables):
    cw = hy_b.shape[-1]
    cb = cw // LANES
    uc = _short_conv(u, u_blk, conv_w, conv_b)
    z = _hyena_conv(uc, 0, uc, 2 * cb, hy_b[0:1], spec, 0, tables)
    return _hyena_conv(uc, cb, z, 0, hy_b[1:2], spec, cb, tables)


def kernel(x, c, ctx, c_ctx, mod_w, mod_b, norm_w, ffn_w1, ffn_w3, ffn_w2, ab_w_in, ab_w_out, diff_lambda,
           diff_subln_w, hy_conv_w, hy_conv_b, hy_f_w0, hy_f_b0, hy_f_w1, hy_f_b1, hy_f_freq, hy_f_wout,
           hy_bias, na_w_in, na_w_out, na_rpb, final_norm_w):
    bsz, n, d = x.shape
    n_ctx = ctx.shape[1]
    depth = mod_w.shape[0]
    a_width = A_HEADS * A_V_DIM
    b_width = hy_bias.shape[-1]
    c_width = C_HEADS * C_HEAD_DIM
    qk_scale = A_QK_DIM ** -0.5
    na_scale = C_HEAD_DIM ** -0.5

    mod = _modulation(c, c_ctx, mod_w, mod_b)
    lat_row = lambda b: b
    ctx_row = lambda b: bsz
    w1, w3, w2 = ffn_w1.astype(BF16), ffn_w3.astype(BF16), ffn_w2.astype(BF16)

    lat, cx = x, ctx
    for layer in range(depth):
        need_ctx = layer < depth - 1
        last = layer == depth - 1
        m, g = mod[layer], norm_w[layer]
        i = layer // 2
        lat = _ffn(lat, m, lat_row, g, w1[layer, 0], w3[layer, 0], w2[layer, 0], 0)
        cx = _ffn(cx, m, ctx_row, g, w1[layer, 0], w3[layer, 0], w2[layer, 0], 0)
        if layer % 2 == 0:
            w_in = ab_w_in[i].astype(BF16)
            w_out = ab_w_out[i].astype(BF16)
            lam_init = 0.8 - 0.6 * math.exp(-0.3 * layer)
            seg = [(a_width, qk_scale, True), (a_width, 1.0, True), (a_width, 1.0, False), (3 * b_width, 1.0, False)]
            q_l, k_l, v_l, u_l = _project(lat, m, lat_row, g, w_in, seg, _rope_tables(n, a_width // A_QK_DIM))
            seg_c = [(s[0], s[1], False) for s in seg]
            if need_ctx:
                q_c, k_c, v_c, u_c = _project(cx, m, ctx_row, g, w_in, seg_c)
            else:
                k_c, v_c = _project(cx, m, ctx_row, g, w_in[:, a_width:3 * a_width], seg_c[1:3])
            a_l = _diff_attention(q_l, [k_c, k_l], [v_c, v_l], diff_lambda[i], diff_subln_w[i], lam_init)

            def hy(u, length):
                filt = _hyena_filters(length, hy_f_w0[i], hy_f_b0[i], hy_f_w1[i], hy_f_b1[i], hy_f_freq[i],
                                      hy_f_wout[i], b_width)
                first, second = _filter_taps(filt, b_width)
                if 2 * length // DFT_L < 2 * SUBLANES:
                    fwd, real, inv = _small_tables(length)
                    spec = _filter_spectrum_small(first, second, real)
                    tables = (fwd, inv)
                else:
                    t1d, t1r, t2, t2i, t3 = _dft_tables(length)
                    spec = _filter_spectrum(first, second, t1r, t2)
                    tables = (t1d, t2, t2i, t3)
                return _hyena(u, 0, hy_conv_w[i], hy_conv_b[i], hy_bias[i], spec, tables)

            b_l = hy(u_l, n)
            pre_l = ([a_l, b_l], w_out)
            if need_ctx:
                a_c = _diff_attention(q_c, [k_c], [v_c], diff_lambda[i], diff_subln_w[i], lam_init)
                b_c = hy(u_c, n_ctx)
                pre_c = ([a_c, b_c], w_out)
        else:
            w_in = na_w_in[i].astype(BF16)
            w_out = na_w_out[i].astype(BF16)
            seg = [(c_width, na_scale, False), (c_width, 1.0, False), (c_width, 1.0, False)]
            q_l, k_l, v_l = _project(lat, m, lat_row, g, w_in, seg)
            if need_ctx:
                raise NotImplementedError("context outputs of the neighbourhood mixer are not needed at this depth")
            k_c, v_c = _project(cx, m, ctx_row, g, w_in[:, c_width:], seg[1:])
            o_l = _neighbourhood_attention(q_l, k_l, v_l, k_c, v_c, na_rpb[i])
            pre_l = ([o_l], w_out)
        lat = _ffn(lat, m, lat_row, g, w1[layer, 1], w3[layer, 1], w2[layer, 1], 2, pre=pre_l,
                   final_w=final_norm_w if last else None)
        if need_ctx:
            cx = _ffn(cx, m, ctx_row, g, w1[layer, 1], w3[layer, 1], w2[layer, 1], 2, pre=pre_c)
    return lat
```

```python
import functools
import math

import jax
import jax.numpy as jnp
from jax import lax
from jax.experimental import pallas as pl
from jax.experimental.pallas import tpu as pltpu

F32 = jnp.float32
BF16 = jnp.bfloat16
HIGHEST = lax.Precision.HIGHEST

GRID_W = 64
N_MOD = 9
NORM_EPS = 1e-6
A_HEADS = 4
A_QK_DIM = 64
A_V_DIM = 128
HY_EMB = 33
HY_BANDS = 16
HY_TARGET = 1e-2
HY_MAX_DECAY = math.log(HY_TARGET) / 0.3
HY_MIN_DECAY = math.log(HY_TARGET) / 1.5
C_HEADS = 16
C_HEAD_DIM = 64
NA_ROWS = 8
NA_COLS = 16
ROPE_BASE = 10000.0

LANES = 128
SUBLANES = 8
VMEM_LIMIT = 56 * 2**20

NA_QROWS = 4
NA_KROWS = NA_QROWS + NA_ROWS
NEG = -1e30
ATTN_CHUNK = 512
SCORE_ROWS = 64
LOG2E = math.log2(math.e)

DFT_L = 128
P1 = DFT_L + 8
LOOP_UNROLL = 8
SMALL_LOOP_UNROLL = 16


def _cparams(sem):
    return pltpu.CompilerParams(dimension_semantics=sem, vmem_limit_bytes=VMEM_LIMIT)


def _const_spec(shape):
    nd = len(shape)
    return pl.BlockSpec(shape, lambda *_: (0,) * nd, pipeline_mode=pl.Buffered(1))


def _rms(x, w):
    ms = jnp.mean(x * x, axis=-1, keepdims=True)
    return x * lax.rsqrt(ms + NORM_EPS) * w


def _sigmoid(x):
    return 1.0 / (1.0 + jnp.exp(-x))


def _mod_kernel(s_ref, w_ref, b_ref, o_ref):
    s = s_ref[...]
    s = s * _sigmoid(s)
    o_ref[0] = jnp.dot(s, w_ref[0], precision=HIGHEST, preferred_element_type=F32) + b_ref[0]


def _modulation(c, c_ctx, mod_w, mod_b):
    depth, d, nd = mod_w.shape
    bsz = c.shape[0]
    rows = -(-(bsz + 1) // SUBLANES) * SUBLANES
    s = jnp.zeros((rows, d), F32).at[:bsz].set(c).at[bsz].set(c_ctx)
    tn = 1024
    out = pl.pallas_call(
        _mod_kernel,
        name="modulation",
        grid=(depth, nd // tn),
        in_specs=[
            pl.BlockSpec((rows, d), lambda l, j: (0, 0)),
            pl.BlockSpec((1, d, tn), lambda l, j: (l, 0, j)),
            pl.BlockSpec((1, 1, tn), lambda l, j: (l, 0, j)),
        ],
        out_specs=pl.BlockSpec((1, rows, tn), lambda l, j: (l, 0, j)),
        out_shape=jax.ShapeDtypeStruct((depth, rows, nd), F32),
        compiler_params=_cparams(("arbitrary", "arbitrary")),
    )(s, mod_w, mod_b.reshape(depth, 1, nd))
    return out[:, :bsz + 1].reshape(depth, bsz + 1, N_MOD, d)


def _ffn_kernel(*refs, k, pre_widths, final):
    it = iter(refs)
    x_ref, mod_ref, g_ref, w1_ref, w3_ref, w2_ref = (next(it) for _ in range(6))
    y_refs = [next(it) for _ in pre_widths]
    wo_ref = next(it) if pre_widths else None
    fw_ref = next(it) if final else None
    o_ref = next(it)

    x = x_ref[0]
    if pre_widths:
        off = 0
        y = None
        for y_ref, w in zip(y_refs, pre_widths):
            t = jnp.dot(y_ref[0], wo_ref[off:off + w, :], preferred_element_type=F32)
            y = t if y is None else y + t
            off += w
        x = x + mod_ref[0, 5:6, :] * y
    shift = mod_ref[0, 3 * k:3 * k + 1, :]
    scale = mod_ref[0, 3 * k + 1:3 * k + 2, :]
    gate = mod_ref[0, 3 * k + 2:3 * k + 3, :]
    hn = (_rms(x, g_ref[k:k + 1, :]) * (1.0 + scale) + shift).astype(BF16)
    a = jnp.dot(hn, w1_ref[...], preferred_element_type=F32)
    b = jnp.dot(hn, w3_ref[...], preferred_element_type=F32)
    act = (a * _sigmoid(a) * b).astype(BF16)
    out = x + 0.5 * gate * jnp.dot(act, w2_ref[...], preferred_element_type=F32)
    if final:
        out = _rms(out, fw_ref[...])
    o_ref[0] = out


def _ffn(x, mod, mod_row, g, w1, w3, w2, k, pre=None, final_w=None):
    bsz, n, d = x.shape
    d_ff = w1.shape[1]
    tm = min(n, 512)
    row = lambda b, i: (b, i, 0)
    in_specs = [
        pl.BlockSpec((1, tm, d), row),
        pl.BlockSpec((1, N_MOD, d), lambda b, i: (mod_row(b), 0, 0)),
        _const_spec(g.shape),
        _const_spec((d, d_ff)),
        _const_spec((d, d_ff)),
        _const_spec((d_ff, d)),
    ]
    args = [x, mod, g, w1, w3, w2]
    pre_widths = ()
    if pre is not None:
        ys, wo = pre
        pre_widths = tuple(y.shape[-1] for y in ys)
        for y in ys:
            in_specs.append(pl.BlockSpec((1, tm, y.shape[-1]), row))
            args.append(y)
        in_specs.append(_const_spec(wo.shape))
        args.append(wo)
    if final_w is not None:
        in_specs.append(_const_spec((1, d)))
        args.append(final_w.reshape(1, d))
    return pl.pallas_call(
        functools.partial(_ffn_kernel, k=k, pre_widths=pre_widths, final=final_w is not None),
        name="macaron_ffn",
        grid=(bsz, n // tm),
        in_specs=in_specs,
        out_specs=pl.BlockSpec((1, tm, d), row),
        out_shape=jax.ShapeDtypeStruct((bsz, n, d), F32),
        compiler_params=_cparams(("parallel", "parallel")),
    )(*args)


def _rope(u, cos, sin_signed, axis):
    w = u.shape[axis]
    idx = lax.broadcasted_iota(jnp.int32, u.shape, axis)
    partner = jnp.where(idx % 32 < 16, pltpu.roll(u, w - 16, axis), pltpu.roll(u, 16, axis))
    return u * cos + partner * sin_signed


def _proj_kernel(*refs, segments, use_rope, any_major):
    it = iter(refs)
    x_ref, mod_ref, g_ref, w_ref = (next(it) for _ in range(4))
    wt_ref = next(it) if any_major else None
    cos_ref, sin_ref, cost_ref, sint_ref = (next(it) if use_rope else None for _ in range(4))
    out_refs = list(it)
    x = x_ref[0]
    hn = (_rms(x, g_ref[1:2, :]) * (1.0 + mod_ref[0, 4:5, :]) + mod_ref[0, 3:4, :]).astype(BF16)
    off = 0
    toff = 0
    for o_ref, (width, scale, rope, tchunk) in zip(out_refs, segments):
        if tchunk:
            ut = lax.dot_general(wt_ref[toff:toff + width, :], hn, (((1,), (1,)), ((), ())),
                                 preferred_element_type=F32)
            if rope:
                ut = _rope(ut, cost_ref[...], sint_ref[...], 0)
            if scale != 1.0:
                ut = ut * scale
            ut = ut.astype(o_ref.dtype)
            for j in range(ut.shape[1] // tchunk):
                o_ref[0, j] = ut[:, j * tchunk:(j + 1) * tchunk]
            toff += width
        else:
            u = jnp.dot(hn, w_ref[:, off:off + width], preferred_element_type=F32)
            if rope:
                u = _rope(u, cos_ref[...], sin_ref[...], 1)
            if scale != 1.0:
                u = u * scale
            o_ref[0] = u.astype(o_ref.dtype)
        off += width


def _project(x, mod, mod_row, g, w, segments, rope_tables=None):
    bsz, n, d = x.shape
    tm = min(n, 512)
    row = lambda b, i: (b, i, 0)
    out_specs, out_shape, wt_parts = [], [], []
    off = 0
    for width, _, _, tchunk in segments:
        if tchunk:
            tchunk = min(tchunk, tm)
            out_specs.append(pl.BlockSpec((1, tm // tchunk, width, tchunk), lambda b, i: (b, i, 0, 0)))
            out_shape.append(jax.ShapeDtypeStruct((bsz, n // tchunk, width, tchunk), BF16))
            wt_parts.append(w[:, off:off + width].T)
        else:
            out_specs.append(pl.BlockSpec((1, tm, width), row))
            out_shape.append(jax.ShapeDtypeStruct((bsz, n, width), BF16))
        off += width
    segments = tuple((w_, s_, r_, min(t_, tm)) for w_, s_, r_, t_ in segments)
    in_specs = [
        pl.BlockSpec((1, tm, d), row),
        pl.BlockSpec((1, N_MOD, d), lambda b, i: (mod_row(b), 0, 0)),
        _const_spec(g.shape),
        _const_spec(w.shape),
    ]
    args = [x, mod, g, w]
    if wt_parts:
        wt = jnp.concatenate(wt_parts, axis=0)
        in_specs.append(_const_spec(wt.shape))
        args.append(wt)
    if rope_tables is not None:
        for t in rope_tables:
            in_specs.append(pl.BlockSpec((tm, t.shape[1]), lambda b, i: (i, 0)))
            args.append(t)
        for t in rope_tables:
            in_specs.append(pl.BlockSpec((t.shape[1], tm), lambda b, i: (0, i)))
            args.append(t.T)
    return pl.pallas_call(
        functools.partial(_proj_kernel, segments=segments, use_rope=rope_tables is not None,
                          any_major=bool(wt_parts)),
        name="mixer_projection",
        grid=(bsz, n // tm),
        in_specs=in_specs,
        out_specs=out_specs,
        out_shape=out_shape,
        compiler_params=_cparams(("parallel", "parallel")),
    )(*args)


def _rope_tables(n, reps):
    t = jnp.arange(n)
    nf = A_QK_DIM // 4
    inv_freq = ROPE_BASE ** (-jnp.arange(nf, dtype=F32) / nf)
    cols, sins = [], []
    for pos in (t // GRID_W, t % GRID_W):
        ang = pos.astype(F32)[:, None] * inv_freq
        c, s = jnp.cos(ang), jnp.sin(ang)
        cols += [c, c]
        sins += [-s, s]
    return jnp.tile(jnp.concatenate(cols, 1), (1, reps)), jnp.tile(jnp.concatenate(sins, 1), (1, reps))


def _scores(s_ref, k, qts):
    for i, qt in enumerate(qts):
        s_ref[i, 0:k.shape[0], :] = jnp.dot(k, qt, preferred_element_type=F32)


def _col_max(s_ref, rows, m):
    for r in range(0, rows, SCORE_ROWS):
        m = jnp.maximum(m, jnp.max(s_ref[r:r + SCORE_ROWS, :], axis=0, keepdims=True))
    return m


def _exp_rows(s_ref, e_ref, rows, m):
    for r in range(0, rows, SCORE_ROWS):
        e_ref[r:r + SCORE_ROWS, :] = jnp.exp2(s_ref[r:r + SCORE_ROWS, :] - m).astype(BF16)


def _fold_chunk(state, s_ref, e_ref, vt, ones):
    kc = vt.shape[1]
    out = []
    for i, (m, acc, l) in enumerate(state):
        m_new = _col_max(s_ref.at[i], kc, m)
        alpha = jnp.exp2(m - m_new)
        _exp_rows(s_ref.at[i], e_ref.at[i], kc, m_new)
        e = e_ref[i, 0:kc, :]
        acc = acc * alpha + jnp.dot(vt, e, preferred_element_type=F32)
        l = l * alpha + jnp.dot(ones, e, preferred_element_type=F32)
        out.append((m_new, acc, l))
    return tuple(out)


def _dattn_kernel(*refs, n_src, lam_init):
    lam_ref, sw_ref, qt_ref = refs[:3]
    k_refs = refs[3:3 + n_src]
    vt_refs = refs[3 + n_src:3 + 2 * n_src]
    o_ref, sa_ref, sb_ref, e_ref = refs[3 + 2 * n_src:]
    qt = qt_ref[0, 0]
    dv, tq = qt.shape
    row = lax.broadcasted_iota(jnp.int32, qt.shape, 0)
    zero = jnp.zeros_like(qt)
    qts = (jnp.where(row < A_QK_DIM, qt, zero), jnp.where(row >= A_QK_DIM, qt, zero))
    init = (jnp.full((1, tq), NEG, F32), jnp.zeros((dv, tq), F32), jnp.zeros((2 * SUBLANES, tq), F32))
    state = (init, init)
    for k_ref, vt_ref in zip(k_refs, vt_refs):
        n_chunks, _, kc = vt_ref.shape[1:]
        ones = jnp.ones((2 * SUBLANES, kc), BF16)
        if n_chunks == 1:
            _scores(sa_ref, k_ref[0], qts)
            state = _fold_chunk(state, sa_ref, e_ref, vt_ref[0, 0], ones)
        else:
            assert n_chunks % 2 == 0

            def key_chunk(c, k_ref=k_ref, kc=kc):
                return k_ref[0, pl.ds(pl.multiple_of(c * kc, kc), kc), :]

            def body(j, st, vt_ref=vt_ref, ones=ones, key_chunk=key_chunk):
                _scores(sb_ref, key_chunk(2 * j + 1), qts)
                st = _fold_chunk(st, sa_ref, e_ref, vt_ref[0, 2 * j], ones)
                _scores(sa_ref, key_chunk(2 * j + 2), qts)
                return _fold_chunk(st, sb_ref, e_ref, vt_ref[0, 2 * j + 1], ones)

            _scores(sa_ref, key_chunk(0), qts)
            state = lax.fori_loop(0, n_chunks // 2 - 1, body, state)
            _scores(sb_ref, key_chunk(n_chunks - 1), qts)
            state = _fold_chunk(state, sa_ref, e_ref, vt_ref[0, n_chunks - 2], ones)
            state = _fold_chunk(state, sb_ref, e_ref, vt_ref[0, n_chunks - 1], ones)
    (_, acc0, l0), (_, acc1, l1) = state
    lp = lam_ref[...]
    lam = (jnp.exp(jnp.sum(lp[0:1] * lp[1:2], axis=-1, keepdims=True))
           - jnp.exp(jnp.sum(lp[2:3] * lp[3:4], axis=-1, keepdims=True)) + lam_init)
    ot = acc0 / l0[0:1] - lam * (acc1 / l1[0:1])
    ot = ot * lax.rsqrt(jnp.mean(ot * ot, axis=0, keepdims=True) + NORM_EPS)
    o_ref[0] = (ot.T * (sw_ref[...] * (1.0 - lam_init))).astype(o_ref.dtype)


def _diff_attention(qt, ks, vts, lam_p, subln_w, lam_init):
    bsz, n_tiles, width, tq = qt.shape
    kc_max = max(v.shape[3] for v in vts)
    kv = lambda b, h, i: (b, 0, h)
    in_specs = [
        pl.BlockSpec(lam_p.shape, lambda b, h, i: (0, 0)),
        pl.BlockSpec((1, A_V_DIM), lambda b, h, i: (0, 0)),
        pl.BlockSpec((1, 1, A_V_DIM, tq), lambda b, h, i: (b, i, h, 0)),
    ]
    in_specs += [pl.BlockSpec((1, k.shape[1], LANES), kv) for k in ks]
    in_specs += [pl.BlockSpec((1, v.shape[1], A_V_DIM, v.shape[3]), lambda b, h, i: (b, 0, h, 0)) for v in vts]
    return pl.pallas_call(
        functools.partial(_dattn_kernel, n_src=len(ks), lam_init=lam_init),
        name="diff_attention",
        grid=(bsz, A_HEADS, n_tiles),
        in_specs=in_specs,
        out_specs=pl.BlockSpec((1, tq, LANES), lambda b, h, i: (b, i, h)),
        out_shape=jax.ShapeDtypeStruct((bsz, n_tiles * tq, width), BF16),
        scratch_shapes=[pltpu.VMEM((2, kc_max, tq), F32), pltpu.VMEM((2, kc_max, tq), F32),
                        pltpu.VMEM((2, kc_max, tq), BF16)],
        compiler_params=_cparams(("parallel", "parallel", "parallel")),
    )(lam_p, subln_w.reshape(1, A_V_DIM), qt, *ks, *vts)


def _na_kernel(qt_ref, k_ref, vt_ref, kc_ref, vct_ref, tb_ref, o_ref, s_ref, e_ref, *, rows):
    i = pl.program_id(2)
    r0 = i * NA_QROWS
    start = jnp.clip(r0 - NA_ROWS // 2, 0, rows - NA_KROWS)
    koff = pl.multiple_of(start * GRID_W, 2 * GRID_W)
    kw = k_ref[0, pl.ds(koff, NA_KROWS * GRID_W), :]
    c0 = start // 2
    vwt = jnp.concatenate([vt_ref[0, c0 + j] for j in range(NA_KROWS // 2)], axis=1)
    kc = kc_ref[0]
    vct = jnp.concatenate([vct_ref[0, j] for j in range(vct_ref.shape[1])], axis=1)
    qt = qt_ref[0, 0]
    row = lax.broadcasted_iota(jnp.int32, qt.shape, 0)
    zero = jnp.zeros_like(qt)
    lane_b = lax.broadcasted_iota(jnp.int32, (GRID_W, LANES), 1)
    n_lat, n_all = kw.shape[0], kw.shape[0] + kc.shape[0]
    vt_all = jnp.concatenate([vwt, vct], axis=1)
    ones = jnp.ones((2 * SUBLANES, n_all), BF16)
    qms = (jnp.where(row < C_HEAD_DIM, qt, zero), jnp.where(row >= C_HEAD_DIM, qt, zero))
    for hh, qm in enumerate(qms):
        s_ref[hh, 0:n_lat, :] = jnp.dot(kw, qm, preferred_element_type=F32)
        s_ref[hh, n_lat:n_all, :] = jnp.dot(kc, qm, preferred_element_type=F32)
    outs = []
    for hh in range(2):
        mx = jnp.full((1, qt.shape[1]), NEG, F32)
        for m in range(NA_KROWS):
            kr = start + m
            blocks = []
            for jp in range(NA_QROWS // 2):
                idx = []
                for j in (2 * jp, 2 * jp + 1):
                    r = r0 + j
                    rs = jnp.clip(r - NA_ROWS // 2, 0, rows - NA_ROWS)
                    valid = jnp.logical_and(kr >= rs, kr < rs + NA_ROWS)
                    idx.append(jnp.where(valid, kr - r + NA_ROWS - 1, 2 * NA_ROWS - 1))
                blocks.append(jnp.where(lane_b < GRID_W, tb_ref[hh, idx[0]], tb_ref[hh, idx[1]]))
            blk = s_ref[hh, m * GRID_W:(m + 1) * GRID_W, :] + jnp.concatenate(blocks, axis=1)
            s_ref[hh, m * GRID_W:(m + 1) * GRID_W, :] = blk
            mx = jnp.maximum(mx, jnp.max(blk, axis=0, keepdims=True))
        for r in range(n_lat, n_all, SCORE_ROWS):
            mx = jnp.maximum(mx, jnp.max(s_ref[hh, r:r + SCORE_ROWS, :], axis=0, keepdims=True))
        _exp_rows(s_ref.at[hh], e_ref.at[hh], n_all, mx)
        e = e_ref[hh]
        acc = jnp.dot(vt_all, e, preferred_element_type=F32)
        l = jnp.dot(ones, e, preferred_element_type=F32)
        outs.append(acc / l[0:1])
    o_ref[0] = jnp.where(row < C_HEAD_DIM, outs[0], outs[1]).T.astype(o_ref.dtype)


def _na_bias_table(rpb):
    nh = rpb.shape[0]
    c = jnp.arange(GRID_W)
    cs = jnp.clip(c - NA_COLS // 2, 0, GRID_W - NA_COLS)
    kc = jnp.arange(GRID_W)
    inside = (kc[None, :] >= cs[:, None]) & (kc[None, :] < cs[:, None] + NA_COLS)
    off = jnp.clip(kc[None, :] - c[:, None] + NA_COLS - 1, 0, 2 * NA_COLS - 2)
    tb = jnp.where(inside[None, None], rpb[:, :, off] * LOG2E, NEG)
    tb = jnp.concatenate([tb, jnp.full((nh, 1, GRID_W, GRID_W), NEG, F32)], axis=1)
    tb = jnp.swapaxes(tb, -1, -2)
    return jnp.concatenate([tb, tb], axis=-1)


def _neighbourhood_attention(qt, k, vt, k_ctx, vt_ctx, rpb):
    bsz, n, width = k.shape
    rows = n // GRID_W
    assert rows >= NA_KROWS and rows % NA_QROWS == 0 and NA_QROWS % 2 == 0 and NA_ROWS % 4 == 0
    n_ctx = k_ctx.shape[1]
    tq = NA_QROWS * GRID_W
    n_keys = NA_KROWS * GRID_W + n_ctx
    per_chunk = qt.shape[3] // tq
    assert vt.shape[3] == 2 * GRID_W and per_chunk * tq == qt.shape[3]
    tb = _na_bias_table(rpb)
    kv = lambda b, h, i: (b, 0, h)
    kvt = lambda b, h, i: (b, 0, h, 0)
    return pl.pallas_call(
        functools.partial(_na_kernel, rows=rows),
        name="neighbourhood_attention",
        grid=(bsz, C_HEADS // 2, n // tq),
        in_specs=[
            pl.BlockSpec((1, 1, LANES, tq), lambda b, h, i: (b, i // per_chunk, h, i % per_chunk)),
            pl.BlockSpec((1, n, LANES), kv),
            pl.BlockSpec((1, vt.shape[1], LANES, vt.shape[3]), kvt),
            pl.BlockSpec((1, n_ctx, LANES), kv),
            pl.BlockSpec((1, vt_ctx.shape[1], LANES, vt_ctx.shape[3]), kvt),
            pl.BlockSpec((2, 2 * NA_ROWS, GRID_W, LANES), lambda b, h, i: (h, 0, 0, 0)),
        ],
        out_specs=pl.BlockSpec((1, tq, LANES), lambda b, h, i: (b, i, h)),
        out_shape=jax.ShapeDtypeStruct((bsz, n, width), BF16),
        scratch_shapes=[pltpu.VMEM((2, n_keys, tq), F32), pltpu.VMEM((2, n_keys, tq), BF16)],
        compiler_params=_cparams(("parallel", "parallel", "arbitrary")),
    )(qt, k, vt, k_ctx, vt_ctx, tb)


def _filter_kernel(z_ref, zr_ref, w0_ref, b0_ref, w1_ref, b1_ref, fr_ref, wf_ref, wb_ref, dl_ref,
                   first_ref, second_ref, *, n_inner, cw):
    fr = fr_ref[...]
    tt = z_ref.shape[0]
    row = lax.broadcasted_iota(jnp.int32, (tt, cw), 0) + pl.program_id(0) * tt
    for pos_ref, wo_ref, o_ref, drop_row0 in ((z_ref, wf_ref, first_ref, False), (zr_ref, wb_ref, second_ref, True)):
        z = pos_ref[...]
        hid = jnp.sin(fr * (jnp.dot(z, w0_ref[...], precision=HIGHEST, preferred_element_type=F32) + b0_ref[...]))
        for i in range(n_inner):
            hid = jnp.sin(fr * (jnp.dot(hid, w1_ref[i], precision=HIGHEST, preferred_element_type=F32)
                                + b1_ref[i:i + 1, :]))
        filt = jnp.dot(hid, wo_ref[...], precision=HIGHEST, preferred_element_type=F32)
        window = jnp.exp(-z[:, 0:1] * dl_ref[...])
        if drop_row0:
            window = jnp.where(row == 0, 0.0, window)
        for j in range(filt.shape[1] // cw):
            o_ref[:, j * cw:(j + 1) * cw] = filt[:, j * cw:(j + 1) * cw] * window


def _filter_features(pos, n):
    t = (pos.astype(F32) / (n - 1))[:, None]
    w = (2.0 * math.pi / n) * pos.astype(F32)[:, None]
    bands = jnp.linspace(1e-4, HY_BANDS - 1, HY_BANDS, dtype=F32)[None, :]
    z = jnp.concatenate([t, jnp.cos(bands * w), -jnp.sin(bands * w)], axis=-1)
    return jnp.pad(z, ((0, 0), (0, LANES - HY_EMB)))


def _hyena_filters(n, f_w0, f_b0, f_w1, f_b1, f_freq, f_wout, cw):
    pos = jnp.arange(n, dtype=jnp.int32)
    z = _filter_features(pos, n)
    zr = _filter_features((n - pos) % n, n)
    w0 = jnp.pad(f_w0, ((0, LANES - HY_EMB), (0, 0)))
    deltas = jnp.abs(jnp.linspace(HY_MIN_DECAY, HY_MAX_DECAY, cw, dtype=F32))[None, :]
    hid = f_w0.shape[1]
    orders = f_wout.shape[1] // (2 * cw)
    wo = f_wout.reshape(hid, orders, 2, cw)
    w_fwd = wo[:, :, 0].reshape(hid, orders * cw)
    w_bwd = wo[:, :, 1].reshape(hid, orders * cw)
    tt = min(n, 512)
    full = lambda shape: pl.BlockSpec(shape, lambda i: (0,) * len(shape))
    out_spec = pl.BlockSpec((tt, orders * cw), lambda i: (i, 0))
    out_shape = jax.ShapeDtypeStruct((n, orders * cw), F32)
    return pl.pallas_call(
        functools.partial(_filter_kernel, n_inner=f_w1.shape[0], cw=cw),
        name="hyena_filters",
        grid=(n // tt,),
        in_specs=[
            pl.BlockSpec((tt, LANES), lambda i: (i, 0)),
            pl.BlockSpec((tt, LANES), lambda i: (i, 0)),
            full(w0.shape), full((1, hid)), full(f_w1.shape), full(f_b1.shape), full((1, hid)),
            full(w_fwd.shape), full(w_bwd.shape), full((1, cw)),
        ],
        out_specs=[out_spec, out_spec],
        out_shape=[out_shape, out_shape],
        compiler_params=_cparams(("parallel",)),
    )(z, zr, w0, f_b0.reshape(1, hid), f_w1, f_b1, f_freq.reshape(1, hid), w_fwd, w_bwd, deltas)


def _cis(phase, period, sign):
    ang = (2.0 * math.pi / period) * (phase % period).astype(F32)
    return jnp.cos(ang), sign * jnp.sin(ang)


def _cblock(mr, mi):
    return jnp.concatenate([jnp.concatenate([mr, -mi], -1), jnp.concatenate([mi, mr], -1)], -2)


def _dft_tables(n):
    big = 2 * n
    nb = big // DFT_L
    f = jnp.arange(DFT_L, dtype=jnp.int32)
    k1 = jnp.arange(nb, dtype=jnp.int32)
    blk = jnp.arange(nb, dtype=jnp.int32)
    ph1 = f[:, None, None] * k1[None, :, None] + DFT_L * blk[None, None, :] * k1[None, :, None]
    cr, ci = _cis(ph1, big, -1.0)
    t1_data = _cblock(cr[:, :, :nb // 2], ci[:, :, :nb // 2])
    t1_real = jnp.concatenate([cr, ci], axis=1)
    ph2 = f[:, None] * f[None, :]
    fr, fi = _cis(ph2, DFT_L, -1.0)
    t2 = _cblock(fr, fi)
    t2_inv = _cblock(fr, -fi)
    n1 = jnp.arange(nb // 2, dtype=jnp.int32)
    ph3 = DFT_L * n1[None, :, None] * k1[None, None, :] + f[:, None, None] * k1[None, None, :]
    dr, di = _cis(ph3, big, 1.0)
    t3 = _cblock(dr, di) / big
    return t1_data.astype(BF16), t1_real.astype(BF16), t2.astype(BF16), t2_inv.astype(BF16), t3.astype(BF16)


def _dft_stage1(xa_ref, xb_ref, t1_ref, a_ref, nb):
    half = nb // 2

    def body(f, carry):
        rhs = jnp.concatenate([xa_ref[pl.ds(f, half, stride=DFT_L), :],
                               xb_ref[pl.ds(f, half, stride=DFT_L), :]], axis=0).astype(BF16)
        out = jnp.dot(t1_ref[f], rhs, preferred_element_type=F32)
        a_ref[pl.ds(pl.multiple_of(f * P1, SUBLANES), 2 * nb), :] = out
        return carry

    lax.fori_loop(0, DFT_L, body, 0, unroll=SMALL_LOOP_UNROLL)


def _dft_stage2_in(a_ref, k1, nb):
    return jnp.concatenate([a_ref[pl.ds(k1, DFT_L, stride=P1), :],
                            a_ref[pl.ds(nb + k1, DFT_L, stride=P1), :]], axis=0).astype(BF16)


def _spectrum_kernel(xa_ref, xb_ref, t1_ref, t2_ref, o_ref, a_ref, *, nb):
    _dft_stage1(xa_ref, xb_ref, t1_ref, a_ref, nb)

    def body(k1, carry):
        out = jnp.dot(t2_ref[...], _dft_stage2_in(a_ref, k1, nb), preferred_element_type=F32)
        o_ref[pl.ds(pl.multiple_of(k1 * 2 * DFT_L, 2 * DFT_L), 2 * DFT_L), :] = out
        return carry

    lax.fori_loop(0, nb, body, 0, unroll=LOOP_UNROLL)


def _filter_spectrum(first, second, t1_real, t2):
    n, cols = first.shape
    nb = 2 * n // DFT_L
    col = lambda j: (0, j)
    return pl.pallas_call(
        functools.partial(_spectrum_kernel, nb=nb),
        name="filter_spectrum",
        grid=(cols // LANES,),
        in_specs=[
            pl.BlockSpec((n, LANES), col),
            pl.BlockSpec((n, LANES), col),
            _const_spec(t1_real.shape),
            _const_spec(t2.shape),
        ],
        out_specs=pl.BlockSpec((4 * n, LANES), col),
        out_shape=jax.ShapeDtypeStruct((4 * n, cols), F32),
        scratch_shapes=[pltpu.VMEM((DFT_L * P1, LANES), F32)],
        compiler_params=_cparams(("parallel",)),
    )(first, second, t1_real, t2)


def _short_conv_kernel(u_ref, w_ref, b_ref, o_ref):
    u = u_ref[0].astype(F32)
    n = u.shape[0]
    row = lax.broadcasted_iota(jnp.int32, u.shape, 0)
    prev = jnp.where(row == 0, 0.0, pltpu.roll(u, 1, 0))
    nxt = jnp.where(row == n - 1, 0.0, pltpu.roll(u, n - 1, 0))
    out = prev * w_ref[0:1, :] + u * w_ref[1:2, :] + nxt * w_ref[2:3, :] + b_ref[...]
    o_ref[0] = out.astype(o_ref.dtype)


def _short_conv(u, u_blk, w, b):
    bsz, n, _ = u.shape
    taps, ch = w.shape
    return pl.pallas_call(
        _short_conv_kernel,
        name="short_conv",
        grid=(bsz, ch // LANES),
        in_specs=[
            pl.BlockSpec((1, n, LANES), lambda bi, j: (bi, 0, u_blk + j)),
            pl.BlockSpec((taps, LANES), lambda bi, j: (0, j)),
            pl.BlockSpec((1, LANES), lambda bi, j: (0, j)),
        ],
        out_specs=pl.BlockSpec((1, n, LANES), lambda bi, j: (bi, 0, j)),
        out_shape=jax.ShapeDtypeStruct((bsz, n, ch), BF16),
        compiler_params=_cparams(("parallel", "parallel")),
    )(u, w, b.reshape(1, ch))


def _cmul_store(s, g, rows):
    sr, si = s[:rows], s[rows:]
    gr, gi = g[:rows], g[rows:]
    return jnp.concatenate([sr * gr - si * gi, sr * gi + si * gr], 0).astype(BF16)


def _hyena_conv_kernel(gate_ref, v_ref, skip_ref, g_ref, t1_ref, t2_ref, t2i_ref, t3_ref, o_ref,
                       va_ref, vb_ref, a_ref, p_ref, y_ref, *, nb, p2, p3):
    half = nb // 2
    two_l = 2 * DFT_L
    va_ref[...] = v_ref[0].astype(F32)
    vb_ref[...] = v_ref[1].astype(F32)

    _dft_stage1(va_ref, vb_ref, t1_ref, a_ref, nb)

    def fwd2(k1, carry):
        s = jnp.dot(t2_ref[...], _dft_stage2_in(a_ref, k1, nb), preferred_element_type=F32)
        off = pl.multiple_of(k1 * two_l, two_l)
        p_ref[pl.ds(off, two_l), :] = _cmul_store(s, g_ref[pl.ds(off, two_l), :], DFT_L)
        return carry

    lax.fori_loop(0, nb, fwd2, 0, unroll=LOOP_UNROLL)

    def inv1(k1, carry):
        off = pl.multiple_of(k1 * two_l, two_l)
        out = jnp.dot(t2i_ref[...], p_ref[pl.ds(off, two_l), :], preferred_element_type=F32)
        a_ref[pl.ds(pl.multiple_of(k1 * p2, SUBLANES), two_l), :] = out
        return carry

    lax.fori_loop(0, nb, inv1, 0, unroll=LOOP_UNROLL)

    def inv2(n2, carry):
        rhs = jnp.concatenate([a_ref[pl.ds(n2, nb, stride=p2), :],
                               a_ref[pl.ds(DFT_L + n2, nb, stride=p2), :]], axis=0).astype(BF16)
        out = jnp.dot(t3_ref[n2], rhs, preferred_element_type=F32)
        y_ref[pl.ds(pl.multiple_of(n2 * p3, SUBLANES), nb), :] = out
        return carry

    lax.fori_loop(0, DFT_L, inv2, 0, unroll=SMALL_LOOP_UNROLL)

    skip = skip_ref[...]

    def epilogue(n1, carry):
        off = pl.multiple_of(n1 * DFT_L, DFT_L)
        for b, vs_ref in enumerate((va_ref, vb_ref)):
            y = y_ref[pl.ds(b * half + n1, DFT_L, stride=p3), :]
            gate = gate_ref[b, pl.ds(off, DFT_L), :].astype(F32)
            o_ref[b, pl.ds(off, DFT_L), :] = (gate * (y + skip * vs_ref[pl.ds(off, DFT_L), :])).astype(o_ref.dtype)
        return carry

    lax.fori_loop(0, half, epilogue, 0, unroll=LOOP_UNROLL)


def _pitch(rows):
    p = -(-rows // SUBLANES)
    if p % 2 == 0:
        p += 1
    return p * SUBLANES


def _conv_specs(n, gate_blk, v_blk, spec_rows, spec_blk):
    return [
        pl.BlockSpec((2, n, LANES), lambda j, p: (p, 0, gate_blk + j)),
        pl.BlockSpec((2, n, LANES), lambda j, p: (p, 0, v_blk + j)),
        pl.BlockSpec((1, LANES), lambda j, p: (0, j)),
        pl.BlockSpec((spec_rows, LANES), lambda j, p: (0, spec_blk + j), pipeline_mode=pl.Buffered(1)),
    ]


def _hyena_conv(gate, gate_blk, v, v_blk, skip, spec, spec_blk, tables):
    bsz, n, _ = v.shape
    cw = skip.shape[-1]
    out_shape = jax.ShapeDtypeStruct((bsz, n, cw), BF16)
    out_spec = pl.BlockSpec((2, n, LANES), lambda j, p: (p, 0, j))
    grid = (cw // LANES, bsz // 2)
    params = _cparams(("parallel", "arbitrary"))
    if len(tables) == 2:
        fwd, inv = tables
        return pl.pallas_call(
            _hyena_conv_small_kernel,
            name="long_conv_small",
            grid=grid,
            in_specs=_conv_specs(n, gate_blk, v_blk, 4 * n, spec_blk) + [_const_spec(fwd.shape), _const_spec(inv.shape)],
            out_specs=out_spec,
            out_shape=out_shape,
            compiler_params=params,
        )(gate, v, skip, spec, fwd, inv)
    nb = 2 * n // DFT_L
    t1, t2, t2i, t3 = tables
    p2, p3 = _pitch(2 * DFT_L), _pitch(nb)
    return pl.pallas_call(
        functools.partial(_hyena_conv_kernel, nb=nb, p2=p2, p3=p3),
        name="long_conv",
        grid=grid,
        in_specs=_conv_specs(n, gate_blk, v_blk, 4 * n, spec_blk) + [_const_spec(t.shape) for t in tables],
        out_specs=out_spec,
        out_shape=out_shape,
        scratch_shapes=[
            pltpu.VMEM((n, LANES), F32), pltpu.VMEM((n, LANES), F32),
            pltpu.VMEM((max(DFT_L * P1, nb * p2), LANES), F32),
            pltpu.VMEM((4 * n, LANES), BF16),
            pltpu.VMEM((DFT_L * p3, LANES), F32),
        ],
        compiler_params=params,
    )(gate, v, skip, spec, t1, t2, t2i, t3)


def _small_tables(n):
    big = 2 * n
    k = jnp.arange(big, dtype=jnp.int32)
    fr, fi = _cis(k[:, None] * k[None, :], big, -1.0)
    fwd = _cblock(fr[:, :n], fi[:, :n])
    real = jnp.concatenate([fr, fi], axis=0)
    inv = _cblock(fr[:n, :], -fi[:n, :]) / big
    return fwd.astype(BF16), real.astype(BF16), inv.astype(BF16)


def _spectrum_small_kernel(xa_ref, xb_ref, f_ref, o_ref):
    taps = jnp.concatenate([xa_ref[...], xb_ref[...]], axis=0).astype(BF16)
    o_ref[...] = jnp.dot(f_ref[...], taps, preferred_element_type=F32)


def _filter_spectrum_small(first, second, real):
    n, cols = first.shape
    col = lambda j: (0, j)
    return pl.pallas_call(
        _spectrum_small_kernel,
        name="filter_spectrum_small",
        grid=(cols // LANES,),
        in_specs=[pl.BlockSpec((n, LANES), col), pl.BlockSpec((n, LANES), col), _const_spec(real.shape)],
        out_specs=pl.BlockSpec((4 * n, LANES), col),
        out_shape=jax.ShapeDtypeStruct((4 * n, cols), F32),
        compiler_params=_cparams(("parallel",)),
    )(first, second, real)


def _hyena_conv_small_kernel(gate_ref, v_ref, skip_ref, g_ref, f_ref, i_ref, o_ref):
    n = v_ref.shape[1]
    rhs = jnp.concatenate([v_ref[0], v_ref[1]], axis=0)
    s = jnp.dot(f_ref[...], rhs, preferred_element_type=F32)
    y = jnp.dot(i_ref[...], _cmul_store(s, g_ref[...], 2 * n), preferred_element_type=F32)
    skip = skip_ref[...]
    for b in range(2):
        yb = y[b * n:(b + 1) * n] + skip * v_ref[b].astype(F32)
        o_ref[b] = (gate_ref[b].astype(F32) * yb).astype(o_ref.dtype)


def _hyena(u, u_blk, conv_w, conv_b, hy_b, spec, tables):
    cw = hy_b.shape[-1]
    cb = cw // LANES
    uc = _short_conv(u, u_blk, conv_w, conv_b)
    z = _hyena_conv(uc, 0, uc, 2 * cb, hy_b[0:1], spec, 0, tables)
    return _hyena_conv(uc, cb, z, 0, hy_b[1:2], spec, cb, tables)


def kernel(x, c, ctx, c_ctx, mod_w, mod_b, norm_w, ffn_w1, ffn_w3, ffn_w2, ab_w_in, ab_w_out, diff_lambda,
           diff_subln_w, hy_conv_w, hy_conv_b, hy_f_w0, hy_f_b0, hy_f_w1, hy_f_b1, hy_f_freq, hy_f_wout,
           hy_bias, na_w_in, na_w_out, na_rpb, final_norm_w):
    bsz, n, d = x.shape
    n_ctx = ctx.shape[1]
    depth = mod_w.shape[0]
    a_width = A_HEADS * A_V_DIM
    b_width = hy_bias.shape[-1]
    c_width = C_HEADS * C_HEAD_DIM
    qk_scale = A_QK_DIM ** -0.5 * LOG2E
    na_scale = C_HEAD_DIM ** -0.5 * LOG2E

    mod = _modulation(c, c_ctx, mod_w, mod_b)
    lat_row = lambda b: b
    ctx_row = lambda b: bsz
    w1, w3, w2 = ffn_w1.astype(BF16), ffn_w3.astype(BF16), ffn_w2.astype(BF16)

    lat, cx = x, ctx
    for layer in range(depth):
        need_ctx = layer < depth - 1
        last = layer == depth - 1
        m, g = mod[layer], norm_w[layer]
        i = layer // 2
        lat = _ffn(lat, m, lat_row, g, w1[layer, 0], w3[layer, 0], w2[layer, 0], 0)
        cx = _ffn(cx, m, ctx_row, g, w1[layer, 0], w3[layer, 0], w2[layer, 0], 0)
        if layer % 2 == 0:
            w_in = ab_w_in[i].astype(BF16)
            w_out = ab_w_out[i].astype(BF16)
            lam_init = 0.8 - 0.6 * math.exp(-0.3 * layer)
            seg = [(a_width, qk_scale, True, ATTN_CHUNK), (a_width, 1.0, True, 0), (a_width, 1.0, False, ATTN_CHUNK),
                   (3 * b_width, 1.0, False, 0)]
            q_l, k_l, v_l, u_l = _project(lat, m, lat_row, g, w_in, seg, _rope_tables(n, a_width // A_QK_DIM))
            seg_c = [(s[0], s[1], False, s[3]) for s in seg]
            if need_ctx:
                q_c, k_c, v_c, u_c = _project(cx, m, ctx_row, g, w_in, seg_c)
            else:
                k_c, v_c = _project(cx, m, ctx_row, g, w_in[:, a_width:3 * a_width], seg_c[1:3])
            a_l = _diff_attention(q_l, [k_c, k_l], [v_c, v_l], diff_lambda[i], diff_subln_w[i], lam_init)

            def hy(u, length):
                first, second = _hyena_filters(length, hy_f_w0[i], hy_f_b0[i], hy_f_w1[i], hy_f_b1[i],
                                               hy_f_freq[i], hy_f_wout[i], b_width)
                if 2 * length // DFT_L < 2 * SUBLANES:
                    fwd, real, inv = _small_tables(length)
                    spec = _filter_spectrum_small(first, second, real)
                    tables = (fwd, inv)
                else:
                    t1d, t1r, t2, t2i, t3 = _dft_tables(length)
                    spec = _filter_spectrum(first, second, t1r, t2)
                    tables = (t1d, t2, t2i, t3)
                return _hyena(u, 0, hy_conv_w[i], hy_conv_b[i], hy_bias[i], spec, tables)

            b_l = hy(u_l, n)
            pre_l = ([a_l, b_l], w_out)
            if need_ctx:
                a_c = _diff_attention(q_c, [k_c], [v_c], diff_lambda[i], diff_subln_w[i], lam_init)
                b_c = hy(u_c, n_ctx)
                pre_c = ([a_c, b_c], w_out)
        else:
            w_in = na_w_in[i].astype(BF16)
            w_out = na_w_out[i].astype(BF16)
            seg = [(c_width, na_scale, False, ATTN_CHUNK), (c_width, 1.0, False, 0), (c_width, 1.0, False, 2 * GRID_W)]
            q_l, k_l, v_l = _project(lat, m, lat_row, g, w_in, seg)
            if need_ctx:
                raise NotImplementedError("context outputs of the neighbourhood mixer are not needed at this depth")
            k_c, v_c = _project(cx, m, ctx_row, g, w_in[:, c_width:], seg[1:])
            o_l = _neighbourhood_attention(q_l, k_l, v_l, k_c, v_c, na_rpb[i])
            pre_l = ([o_l], w_out)
        lat = _ffn(lat, m, lat_row, g, w1[layer, 1], w3[layer, 1], w2[layer, 1], 2, pre=pre_l,
                   final_w=final_norm_w if last else None)
        if need_ctx:
            cx = _ffn(cx, m, ctx_row, g, w1[layer, 1], w3[layer, 1], w2[layer, 1], 2, pre=pre_c)
    return lat
```

```python
import functools
import math

import jax
import jax.numpy as jnp
from jax import lax
from jax.experimental import pallas as pl
from jax.experimental.pallas import tpu as pltpu

F32 = jnp.float32
BF16 = jnp.bfloat16
HIGHEST = lax.Precision.HIGHEST

GRID_W = 64
N_MOD = 9
NORM_EPS = 1e-6
A_HEADS = 4
A_QK_DIM = 64
A_V_DIM = 128
HY_EMB = 33
HY_BANDS = 16
HY_TARGET = 1e-2
HY_MAX_DECAY = math.log(HY_TARGET) / 0.3
HY_MIN_DECAY = math.log(HY_TARGET) / 1.5
C_HEADS = 16
C_HEAD_DIM = 64
NA_ROWS = 8
NA_COLS = 16
ROPE_BASE = 10000.0

LANES = 128
SUBLANES = 8
VMEM_LIMIT = 56 * 2**20

NA_QROWS = 4
NA_KROWS = NA_QROWS + NA_ROWS
NEG = -1e30
ATTN_CHUNK = 512
SCORE_ROWS = 64
LOG2E = math.log2(math.e)

DFT_L = 128
P1 = DFT_L + 8
LOOP_UNROLL = 8
SMALL_LOOP_UNROLL = 16


def _cparams(sem):
    return pltpu.CompilerParams(dimension_semantics=sem, vmem_limit_bytes=VMEM_LIMIT)


def _const_spec(shape):
    nd = len(shape)
    return pl.BlockSpec(shape, lambda *_: (0,) * nd, pipeline_mode=pl.Buffered(1))


def _rms(x, w):
    ms = jnp.mean(x * x, axis=-1, keepdims=True)
    return x * lax.rsqrt(ms + NORM_EPS) * w


def _sigmoid(x):
    return 1.0 / (1.0 + jnp.exp(-x))


def _mod_kernel(s_ref, w_ref, b_ref, o_ref):
    s = s_ref[...]
    s = s * _sigmoid(s)
    o_ref[0] = jnp.dot(s, w_ref[0], precision=HIGHEST, preferred_element_type=F32) + b_ref[0]


def _modulation(c, c_ctx, mod_w, mod_b):
    depth, d, nd = mod_w.shape
    bsz = c.shape[0]
    rows = -(-(bsz + 1) // SUBLANES) * SUBLANES
    s = jnp.zeros((rows, d), F32).at[:bsz].set(c).at[bsz].set(c_ctx)
    tn = 1024
    out = pl.pallas_call(
        _mod_kernel,
        name="modulation",
        grid=(depth, nd // tn),
        in_specs=[
            pl.BlockSpec((rows, d), lambda l, j: (0, 0)),
            pl.BlockSpec((1, d, tn), lambda l, j: (l, 0, j)),
            pl.BlockSpec((1, 1, tn), lambda l, j: (l, 0, j)),
        ],
        out_specs=pl.BlockSpec((1, rows, tn), lambda l, j: (l, 0, j)),
        out_shape=jax.ShapeDtypeStruct((depth, rows, nd), F32),
        compiler_params=_cparams(("arbitrary", "arbitrary")),
    )(s, mod_w, mod_b.reshape(depth, 1, nd))
    return out[:, :bsz + 1].reshape(depth, bsz + 1, N_MOD, d)


def _ffn_kernel(*refs, k, pre_widths, final):
    it = iter(refs)
    x_ref, mod_ref, g_ref, w1_ref, w3_ref, w2_ref = (next(it) for _ in range(6))
    y_refs = [next(it) for _ in pre_widths]
    wo_ref = next(it) if pre_widths else None
    fw_ref = next(it) if final else None
    o_ref = next(it)

    x = x_ref[0]
    if pre_widths:
        off = 0
        y = None
        for y_ref, w in zip(y_refs, pre_widths):
            t = jnp.dot(y_ref[0], wo_ref[off:off + w, :], preferred_element_type=F32)
            y = t if y is None else y + t
            off += w
        x = x + mod_ref[0, 5:6, :] * y
    shift = mod_ref[0, 3 * k:3 * k + 1, :]
    scale = mod_ref[0, 3 * k + 1:3 * k + 2, :]
    gate = mod_ref[0, 3 * k + 2:3 * k + 3, :]
    hn = (_rms(x, g_ref[k:k + 1, :]) * (1.0 + scale) + shift).astype(BF16)
    a = jnp.dot(hn, w1_ref[...], preferred_element_type=F32)
    b = jnp.dot(hn, w3_ref[...], preferred_element_type=F32)
    act = (a * _sigmoid(a) * b).astype(BF16)
    out = x + 0.5 * gate * jnp.dot(act, w2_ref[...], preferred_element_type=F32)
    if final:
        out = _rms(out, fw_ref[...])
    o_ref[0] = out


def _ffn(x, mod, mod_row, g, w1, w3, w2, widx, k, pre=None, final_w=None):
    bsz, n, d = x.shape
    d_ff = w1.shape[-1]
    tm = min(n, 512)
    row = lambda b, i: (b, i, 0)
    pick = lambda b, i: (widx[0], widx[1], 0, 0)
    in_specs = [
        pl.BlockSpec((1, tm, d), row),
        pl.BlockSpec((1, N_MOD, d), lambda b, i: (mod_row(b), 0, 0)),
        _const_spec(g.shape),
        pl.BlockSpec((None, None, d, d_ff), pick, pipeline_mode=pl.Buffered(1)),
        pl.BlockSpec((None, None, d, d_ff), pick, pipeline_mode=pl.Buffered(1)),
        pl.BlockSpec((None, None, d_ff, d), pick, pipeline_mode=pl.Buffered(1)),
    ]
    args = [x, mod, g, w1, w3, w2]
    pre_widths = ()
    if pre is not None:
        ys, wo = pre
        pre_widths = tuple(y.shape[-1] for y in ys)
        for y in ys:
            in_specs.append(pl.BlockSpec((1, tm, y.shape[-1]), row))
            args.append(y)
        in_specs.append(_const_spec(wo.shape))
        args.append(wo)
    if final_w is not None:
        in_specs.append(_const_spec((1, d)))
        args.append(final_w.reshape(1, d))
    return pl.pallas_call(
        functools.partial(_ffn_kernel, k=k, pre_widths=pre_widths, final=final_w is not None),
        name="macaron_ffn",
        grid=(bsz, n // tm),
        in_specs=in_specs,
        out_specs=pl.BlockSpec((1, tm, d), row),
        out_shape=jax.ShapeDtypeStruct((bsz, n, d), F32),
        compiler_params=_cparams(("parallel", "parallel")),
    )(*args)


def _rope(u, cos, sin_signed, axis):
    w = u.shape[axis]
    idx = lax.broadcasted_iota(jnp.int32, u.shape, axis)
    partner = jnp.where(idx % 32 < 16, pltpu.roll(u, w - 16, axis), pltpu.roll(u, 16, axis))
    return u * cos + partner * sin_signed


def _proj_kernel(*refs, segments, use_rope, any_major):
    it = iter(refs)
    x_ref, mod_ref, g_ref, w_ref = (next(it) for _ in range(4))
    wt_ref = next(it) if any_major else None
    cos_ref, sin_ref, cost_ref, sint_ref = (next(it) if use_rope else None for _ in range(4))
    out_refs = list(it)
    x = x_ref[0]
    hn = (_rms(x, g_ref[1:2, :]) * (1.0 + mod_ref[0, 4:5, :]) + mod_ref[0, 3:4, :]).astype(BF16)
    off = 0
    toff = 0
    for o_ref, (width, scale, rope, tchunk) in zip(out_refs, segments):
        if tchunk:
            ut = lax.dot_general(wt_ref[toff:toff + width, :], hn, (((1,), (1,)), ((), ())),
                                 preferred_element_type=F32)
            if rope:
                ut = _rope(ut, cost_ref[...], sint_ref[...], 0)
            if scale != 1.0:
                ut = ut * scale
            ut = ut.astype(o_ref.dtype)
            for j in range(ut.shape[1] // tchunk):
                o_ref[0, j] = ut[:, j * tchunk:(j + 1) * tchunk]
            toff += width
        else:
            u = jnp.dot(hn, w_ref[:, off:off + width], preferred_element_type=F32)
            if rope:
                u = _rope(u, cos_ref[...], sin_ref[...], 1)
            if scale != 1.0:
                u = u * scale
            o_ref[0] = u.astype(o_ref.dtype)
        off += width


def _project(x, mod, mod_row, g, w, segments, rope_tables=None):
    bsz, n, d = x.shape
    tm = min(n, 512)
    row = lambda b, i: (b, i, 0)
    out_specs, out_shape, wt_parts = [], [], []
    off = 0
    for width, _, _, tchunk in segments:
        if tchunk:
            tchunk = min(tchunk, tm)
            out_specs.append(pl.BlockSpec((1, tm // tchunk, width, tchunk), lambda b, i: (b, i, 0, 0)))
            out_shape.append(jax.ShapeDtypeStruct((bsz, n // tchunk, width, tchunk), BF16))
            wt_parts.append(w[:, off:off + width].T)
        else:
            out_specs.append(pl.BlockSpec((1, tm, width), row))
            out_shape.append(jax.ShapeDtypeStruct((bsz, n, width), BF16))
        off += width
    segments = tuple((w_, s_, r_, min(t_, tm)) for w_, s_, r_, t_ in segments)
    in_specs = [
        pl.BlockSpec((1, tm, d), row),
        pl.BlockSpec((1, N_MOD, d), lambda b, i: (mod_row(b), 0, 0)),
        _const_spec(g.shape),
        _const_spec(w.shape),
    ]
    args = [x, mod, g, w]
    if wt_parts:
        wt = jnp.concatenate(wt_parts, axis=0)
        in_specs.append(_const_spec(wt.shape))
        args.append(wt)
    if rope_tables is not None:
        for t in rope_tables:
            in_specs.append(pl.BlockSpec((tm, t.shape[1]), lambda b, i: (i, 0)))
            args.append(t)
        for t in rope_tables:
            in_specs.append(pl.BlockSpec((t.shape[1], tm), lambda b, i: (0, i)))
            args.append(t.T)
    return pl.pallas_call(
        functools.partial(_proj_kernel, segments=segments, use_rope=rope_tables is not None,
                          any_major=bool(wt_parts)),
        name="mixer_projection",
        grid=(bsz, n // tm),
        in_specs=in_specs,
        out_specs=out_specs,
        out_shape=out_shape,
        compiler_params=_cparams(("parallel", "parallel")),
    )(*args)


def _rope_tables(n, reps):
    t = jnp.arange(n)
    nf = A_QK_DIM // 4
    inv_freq = ROPE_BASE ** (-jnp.arange(nf, dtype=F32) / nf)
    cols, sins = [], []
    for pos in (t // GRID_W, t % GRID_W):
        ang = pos.astype(F32)[:, None] * inv_freq
        c, s = jnp.cos(ang), jnp.sin(ang)
        cols += [c, c]
        sins += [-s, s]
    return jnp.tile(jnp.concatenate(cols, 1), (1, reps)), jnp.tile(jnp.concatenate(sins, 1), (1, reps))


def _exp_rows(s_ref, e_ref, rows, m):
    for r in range(0, rows, SCORE_ROWS):
        e_ref[r:r + SCORE_ROWS, :] = jnp.exp2(s_ref[r:r + SCORE_ROWS, :] - m).astype(BF16)


def _dattn_rows_kernel(*refs, n_src, lam_init):
    lam_ref, sw_ref, q_ref = refs[:3]
    k_refs = refs[3:3 + n_src]
    v_refs = refs[3 + n_src:3 + 2 * n_src]
    o_ref = refs[3 + 2 * n_src]
    q = q_ref[0]
    lane = lax.broadcasted_iota(jnp.int32, q.shape, 1)
    zero = jnp.zeros_like(q)
    nt = (((1,), (1,)), ((), ()))
    qms = (jnp.where(lane < A_QK_DIM, q, zero), jnp.where(lane >= A_QK_DIM, q, zero))
    scores = [[lax.dot_general(qm, k_ref[0], nt, preferred_element_type=F32) for k_ref in k_refs] for qm in qms]
    outs = []
    for ss in scores:
        m = None
        for s in ss:
            ms = jnp.max(s, axis=-1, keepdims=True)
            m = ms if m is None else jnp.maximum(m, ms)
        acc = None
        den = None
        for s, v_ref in zip(ss, v_refs):
            e = jnp.exp2(s - m)
            ds = jnp.sum(e, axis=-1, keepdims=True)
            t = jnp.dot(e.astype(BF16), v_ref[0], preferred_element_type=F32)
            den = ds if den is None else den + ds
            acc = t if acc is None else acc + t
        outs.append(acc / den)
    lp = lam_ref[...]
    lam = (jnp.exp(jnp.sum(lp[0:1] * lp[1:2], axis=-1, keepdims=True))
           - jnp.exp(jnp.sum(lp[2:3] * lp[3:4], axis=-1, keepdims=True)) + lam_init)
    o = outs[0] - lam * outs[1]
    o_ref[0] = (_rms(o, sw_ref[...]) * (1.0 - lam_init)).astype(o_ref.dtype)


def _diff_attention_rows(q, ks, vs, lam_p, subln_w, lam_init):
    bsz, nq, width = q.shape
    tq = min(nq, 256)
    hb = lambda b, h, i: (b, i, h)
    kv = lambda b, h, i: (b, 0, h)
    in_specs = [
        pl.BlockSpec(lam_p.shape, lambda b, h, i: (0, 0)),
        pl.BlockSpec((1, A_V_DIM), lambda b, h, i: (0, 0)),
        pl.BlockSpec((1, tq, LANES), hb),
    ]
    in_specs += [pl.BlockSpec((1, k.shape[1], LANES), kv) for k in ks]
    in_specs += [pl.BlockSpec((1, v.shape[1], LANES), kv) for v in vs]
    return pl.pallas_call(
        functools.partial(_dattn_rows_kernel, n_src=len(ks), lam_init=lam_init),
        name="diff_attention_rows",
        grid=(bsz, A_HEADS, nq // tq),
        in_specs=in_specs,
        out_specs=pl.BlockSpec((1, tq, LANES), hb),
        out_shape=jax.ShapeDtypeStruct((bsz, nq, width), BF16),
        compiler_params=_cparams(("parallel", "parallel", "parallel")),
    )(lam_p, subln_w.reshape(1, A_V_DIM), q, *ks, *vs)


def _na_kernel(qt_ref, k_ref, vt_ref, kc_ref, vct_ref, tb_ref, o_ref, s_ref, e_ref, *, rows):
    i = pl.program_id(2)
    r0 = i * NA_QROWS
    start = jnp.clip(r0 - NA_ROWS // 2, 0, rows - NA_KROWS)
    koff = pl.multiple_of(start * GRID_W, 2 * GRID_W)
    kw = k_ref[0, pl.ds(koff, NA_KROWS * GRID_W), :]
    c0 = start // 2
    vwt = jnp.concatenate([vt_ref[0, c0 + j] for j in range(NA_KROWS // 2)], axis=1)
    kc = kc_ref[0]
    vct = jnp.concatenate([vct_ref[0, j] for j in range(vct_ref.shape[1])], axis=1)
    qt = qt_ref[0, 0]
    row = lax.broadcasted_iota(jnp.int32, qt.shape, 0)
    zero = jnp.zeros_like(qt)
    lane_b = lax.broadcasted_iota(jnp.int32, (GRID_W, LANES), 1)
    n_lat, n_all = kw.shape[0], kw.shape[0] + kc.shape[0]
    vt_all = jnp.concatenate([vwt, vct], axis=1)
    ones = jnp.ones((2 * SUBLANES, n_all), BF16)
    qms = (jnp.where(row < C_HEAD_DIM, qt, zero), jnp.where(row >= C_HEAD_DIM, qt, zero))
    for hh, qm in enumerate(qms):
        s_ref[hh, 0:n_lat, :] = jnp.dot(kw, qm, preferred_element_type=F32)
        s_ref[hh, n_lat:n_all, :] = jnp.dot(kc, qm, preferred_element_type=F32)
    outs = []
    for hh in range(2):
        mx = jnp.full((1, qt.shape[1]), NEG, F32)
        for m in range(NA_KROWS):
            kr = start + m
            blocks = []
            for jp in range(NA_QROWS // 2):
                idx = []
                for j in (2 * jp, 2 * jp + 1):
                    r = r0 + j
                    rs = jnp.clip(r - NA_ROWS // 2, 0, rows - NA_ROWS)
                    valid = jnp.logical_and(kr >= rs, kr < rs + NA_ROWS)
                    idx.append(jnp.where(valid, kr - r + NA_ROWS - 1, 2 * NA_ROWS - 1))
                blocks.append(jnp.where(lane_b < GRID_W, tb_ref[hh, idx[0]], tb_ref[hh, idx[1]]))
            blk = s_ref[hh, m * GRID_W:(m + 1) * GRID_W, :] + jnp.concatenate(blocks, axis=1)
            s_ref[hh, m * GRID_W:(m + 1) * GRID_W, :] = blk
            mx = jnp.maximum(mx, jnp.max(blk, axis=0, keepdims=True))
        for r in range(n_lat, n_all, SCORE_ROWS):
            mx = jnp.maximum(mx, jnp.max(s_ref[hh, r:r + SCORE_ROWS, :], axis=0, keepdims=True))
        _exp_rows(s_ref.at[hh], e_ref.at[hh], n_all, mx)
        e = e_ref[hh]
        acc = jnp.dot(vt_all, e, preferred_element_type=F32)
        l = jnp.dot(ones, e, preferred_element_type=F32)
        outs.append(acc / l[0:1])
    o_ref[0] = jnp.where(row < C_HEAD_DIM, outs[0], outs[1]).T.astype(o_ref.dtype)


def _na_bias_table(rpb):
    nh = rpb.shape[0]
    c = jnp.arange(GRID_W)
    cs = jnp.clip(c - NA_COLS // 2, 0, GRID_W - NA_COLS)
    kc = jnp.arange(GRID_W)
    inside = (kc[None, :] >= cs[:, None]) & (kc[None, :] < cs[:, None] + NA_COLS)
    off = jnp.clip(kc[None, :] - c[:, None] + NA_COLS - 1, 0, 2 * NA_COLS - 2)
    tb = jnp.where(inside[None, None], rpb[:, :, off] * LOG2E, NEG)
    tb = jnp.concatenate([tb, jnp.full((nh, 1, GRID_W, GRID_W), NEG, F32)], axis=1)
    tb = jnp.swapaxes(tb, -1, -2)
    return jnp.concatenate([tb, tb], axis=-1)


def _neighbourhood_attention(qt, k, vt, k_ctx, vt_ctx, rpb):
    bsz, n, width = k.shape
    rows = n // GRID_W
    assert rows >= NA_KROWS and rows % NA_QROWS == 0 and NA_QROWS % 2 == 0 and NA_ROWS % 4 == 0
    n_ctx = k_ctx.shape[1]
    tq = NA_QROWS * GRID_W
    n_keys = NA_KROWS * GRID_W + n_ctx
    per_chunk = qt.shape[3] // tq
    assert vt.shape[3] == 2 * GRID_W and per_chunk * tq == qt.shape[3]
    tb = _na_bias_table(rpb)
    kv = lambda b, h, i: (b, 0, h)
    kvt = lambda b, h, i: (b, 0, h, 0)
    return pl.pallas_call(
        functools.partial(_na_kernel, rows=rows),
        name="neighbourhood_attention",
        grid=(bsz, C_HEADS // 2, n // tq),
        in_specs=[
            pl.BlockSpec((1, 1, LANES, tq), lambda b, h, i: (b, i // per_chunk, h, i % per_chunk)),
            pl.BlockSpec((1, n, LANES), kv),
            pl.BlockSpec((1, vt.shape[1], LANES, vt.shape[3]), kvt),
            pl.BlockSpec((1, n_ctx, LANES), kv),
            pl.BlockSpec((1, vt_ctx.shape[1], LANES, vt_ctx.shape[3]), kvt),
            pl.BlockSpec((2, 2 * NA_ROWS, GRID_W, LANES), lambda b, h, i: (h, 0, 0, 0)),
        ],
        out_specs=pl.BlockSpec((1, tq, LANES), lambda b, h, i: (b, i, h)),
        out_shape=jax.ShapeDtypeStruct((bsz, n, width), BF16),
        scratch_shapes=[pltpu.VMEM((2, n_keys, tq), F32), pltpu.VMEM((2, n_keys, tq), BF16)],
        compiler_params=_cparams(("parallel", "parallel", "arbitrary")),
    )(qt, k, vt, k_ctx, vt_ctx, tb)


def _filter_kernel(z_ref, zr_ref, w0_ref, b0_ref, w1_ref, b1_ref, fr_ref, wf_ref, wb_ref, dl_ref,
                   first_ref, second_ref, *, n_inner, cw):
    fr = fr_ref[...]
    tt = z_ref.shape[0]
    row = lax.broadcasted_iota(jnp.int32, (tt, cw), 0) + pl.program_id(0) * tt
    for pos_ref, wo_ref, o_ref, drop_row0 in ((z_ref, wf_ref, first_ref, False), (zr_ref, wb_ref, second_ref, True)):
        z = pos_ref[...]
        hid = jnp.sin(fr * (jnp.dot(z, w0_ref[...], precision=HIGHEST, preferred_element_type=F32) + b0_ref[...]))
        for i in range(n_inner):
            hid = jnp.sin(fr * (jnp.dot(hid, w1_ref[i], precision=HIGHEST, preferred_element_type=F32)
                                + b1_ref[i:i + 1, :]))
        filt = jnp.dot(hid, wo_ref[...], precision=HIGHEST, preferred_element_type=F32)
        window = jnp.exp(-z[:, 0:1] * dl_ref[...])
        if drop_row0:
            window = jnp.where(row == 0, 0.0, window)
        for j in range(filt.shape[1] // cw):
            o_ref[:, j * cw:(j + 1) * cw] = filt[:, j * cw:(j + 1) * cw] * window


def _filter_features(pos, n):
    t = (pos.astype(F32) / (n - 1))[:, None]
    w = (2.0 * math.pi / n) * pos.astype(F32)[:, None]
    bands = jnp.linspace(1e-4, HY_BANDS - 1, HY_BANDS, dtype=F32)[None, :]
    z = jnp.concatenate([t, jnp.cos(bands * w), -jnp.sin(bands * w)], axis=-1)
    return jnp.pad(z, ((0, 0), (0, LANES - HY_EMB)))


def _hyena_filters(n, f_w0, f_b0, f_w1, f_b1, f_freq, f_wout, cw):
    pos = jnp.arange(n, dtype=jnp.int32)
    z = _filter_features(pos, n)
    zr = _filter_features((n - pos) % n, n)
    w0 = jnp.pad(f_w0, ((0, LANES - HY_EMB), (0, 0)))
    deltas = jnp.abs(jnp.linspace(HY_MIN_DECAY, HY_MAX_DECAY, cw, dtype=F32))[None, :]
    hid = f_w0.shape[1]
    orders = f_wout.shape[1] // (2 * cw)
    wo = f_wout.reshape(hid, orders, 2, cw)
    w_fwd = wo[:, :, 0].reshape(hid, orders * cw)
    w_bwd = wo[:, :, 1].reshape(hid, orders * cw)
    tt = min(n, 512)
    full = lambda shape: pl.BlockSpec(shape, lambda i: (0,) * len(shape))
    out_spec = pl.BlockSpec((tt, orders * cw), lambda i: (i, 0))
    out_shape = jax.ShapeDtypeStruct((n, orders * cw), F32)
    return pl.pallas_call(
        functools.partial(_filter_kernel, n_inner=f_w1.shape[0], cw=cw),
        name="hyena_filters",
        grid=(n // tt,),
        in_specs=[
            pl.BlockSpec((tt, LANES), lambda i: (i, 0)),
            pl.BlockSpec((tt, LANES), lambda i: (i, 0)),
            full(w0.shape), full((1, hid)), full(f_w1.shape), full(f_b1.shape), full((1, hid)),
            full(w_fwd.shape), full(w_bwd.shape), full((1, cw)),
        ],
        out_specs=[out_spec, out_spec],
        out_shape=[out_shape, out_shape],
        compiler_params=_cparams(("parallel",)),
    )(z, zr, w0, f_b0.reshape(1, hid), f_w1, f_b1, f_freq.reshape(1, hid), w_fwd, w_bwd, deltas)


def _cis(phase, period, sign):
    ang = (2.0 * math.pi / period) * (phase % period).astype(F32)
    return jnp.cos(ang), sign * jnp.sin(ang)


def _cblock(mr, mi):
    return jnp.concatenate([jnp.concatenate([mr, -mi], -1), jnp.concatenate([mi, mr], -1)], -2)


def _dft_tables(n):
    big = 2 * n
    nb = big // DFT_L
    f = jnp.arange(DFT_L, dtype=jnp.int32)
    k1 = jnp.arange(nb, dtype=jnp.int32)
    blk = jnp.arange(nb, dtype=jnp.int32)
    ph1 = f[:, None, None] * k1[None, :, None] + DFT_L * blk[None, None, :] * k1[None, :, None]
    cr, ci = _cis(ph1, big, -1.0)
    t1_data = _cblock(cr[:, :, :nb // 2], ci[:, :, :nb // 2])
    t1_real = jnp.concatenate([cr, ci], axis=1)
    ph2 = f[:, None] * f[None, :]
    fr, fi = _cis(ph2, DFT_L, -1.0)
    t2 = _cblock(fr, fi)
    t2_inv = _cblock(fr, -fi)
    n1 = jnp.arange(nb // 2, dtype=jnp.int32)
    ph3 = DFT_L * n1[None, :, None] * k1[None, None, :] + f[:, None, None] * k1[None, None, :]
    dr, di = _cis(ph3, big, 1.0)
    t3 = _cblock(dr, di) / big
    return t1_data.astype(BF16), t1_real.astype(BF16), t2.astype(BF16), t2_inv.astype(BF16), t3.astype(BF16)


def _dft_stage1(xa_ref, xb_ref, t1_ref, a_ref, nb):
    half = nb // 2

    def body(f, carry):
        rhs = jnp.concatenate([xa_ref[pl.ds(f, half, stride=DFT_L), :],
                               xb_ref[pl.ds(f, half, stride=DFT_L), :]], axis=0).astype(BF16)
        out = jnp.dot(t1_ref[f], rhs, preferred_element_type=F32)
        a_ref[pl.ds(pl.multiple_of(f * P1, SUBLANES), 2 * nb), :] = out
        return carry

    lax.fori_loop(0, DFT_L, body, 0, unroll=SMALL_LOOP_UNROLL)


def _dft_stage2_in(a_ref, k1, nb):
    return jnp.concatenate([a_ref[pl.ds(k1, DFT_L, stride=P1), :],
                            a_ref[pl.ds(nb + k1, DFT_L, stride=P1), :]], axis=0).astype(BF16)


def _spectrum_kernel(xa_ref, xb_ref, t1_ref, t2_ref, o_ref, a_ref, *, nb):
    _dft_stage1(xa_ref, xb_ref, t1_ref, a_ref, nb)

    def body(k1, carry):
        out = jnp.dot(t2_ref[...], _dft_stage2_in(a_ref, k1, nb), preferred_element_type=F32)
        o_ref[pl.ds(pl.multiple_of(k1 * 2 * DFT_L, 2 * DFT_L), 2 * DFT_L), :] = out
        return carry

    lax.fori_loop(0, nb, body, 0, unroll=LOOP_UNROLL)


def _filter_spectrum(first, second, t1_real, t2):
    n, cols = first.shape
    nb = 2 * n // DFT_L
    col = lambda j: (0, j)
    return pl.pallas_call(
        functools.partial(_spectrum_kernel, nb=nb),
        name="filter_spectrum",
        grid=(cols // LANES,),
        in_specs=[
            pl.BlockSpec((n, LANES), col),
            pl.BlockSpec((n, LANES), col),
            _const_spec(t1_real.shape),
            _const_spec(t2.shape),
        ],
        out_specs=pl.BlockSpec((4 * n, LANES), col),
        out_shape=jax.ShapeDtypeStruct((4 * n, cols), F32),
        scratch_shapes=[pltpu.VMEM((DFT_L * P1, LANES), F32)],
        compiler_params=_cparams(("parallel",)),
    )(first, second, t1_real, t2)


def _short_conv_kernel(u_ref, w_ref, b_ref, o_ref):
    u = u_ref[0].astype(F32)
    n = u.shape[0]
    w0, w1, w2 = w_ref[0:1, :], w_ref[1:2, :], w_ref[2:3, :]
    out = pltpu.roll(u, 1, 0) * w0 + u * w1 + pltpu.roll(u, n - 1, 0) * w2 + b_ref[...]
    row = lax.broadcasted_iota(jnp.int32, (SUBLANES, u.shape[1]), 0)
    head = out[0:SUBLANES] - jnp.where(row == 0, u[n - 1:n] * w0, 0.0)
    tail = out[n - SUBLANES:n] - jnp.where(row == SUBLANES - 1, u[0:1] * w2, 0.0)
    o_ref[0] = jnp.concatenate([head, out[SUBLANES:n - SUBLANES], tail], axis=0).astype(o_ref.dtype)


def _short_conv(u, u_blk, w, b):
    bsz, n, _ = u.shape
    taps, ch = w.shape
    cb = 2 * LANES
    assert ch % cb == 0 and (u_blk * LANES) % cb == 0
    first = u_blk * LANES // cb
    return pl.pallas_call(
        _short_conv_kernel,
        name="short_conv",
        grid=(bsz, ch // cb),
        in_specs=[
            pl.BlockSpec((1, n, cb), lambda bi, j: (bi, 0, first + j)),
            pl.BlockSpec((taps, cb), lambda bi, j: (0, j)),
            pl.BlockSpec((1, cb), lambda bi, j: (0, j)),
        ],
        out_specs=pl.BlockSpec((1, n, cb), lambda bi, j: (bi, 0, j)),
        out_shape=jax.ShapeDtypeStruct((bsz, n, ch), BF16),
        compiler_params=_cparams(("parallel", "parallel")),
    )(u, w, b.reshape(1, ch))


def _cmul_store(s, g, rows):
    sr, si = s[:rows], s[rows:]
    gr, gi = g[:rows], g[rows:]
    return jnp.concatenate([sr * gr - si * gi, sr * gi + si * gr], 0).astype(BF16)


def _hyena_conv_kernel(gate_ref, v_ref, skip_ref, g_ref, t1_ref, t2_ref, t2i_ref, t3_ref, o_ref,
                       va_ref, vb_ref, a_ref, p_ref, y_ref, *, nb, p2, p3):
    half = nb // 2
    two_l = 2 * DFT_L
    va_ref[...] = v_ref[0].astype(F32)
    vb_ref[...] = v_ref[1].astype(F32)

    _dft_stage1(va_ref, vb_ref, t1_ref, a_ref, nb)

    def fwd2(k1, carry):
        s = jnp.dot(t2_ref[...], _dft_stage2_in(a_ref, k1, nb), preferred_element_type=F32)
        off = pl.multiple_of(k1 * two_l, two_l)
        p_ref[pl.ds(off, two_l), :] = _cmul_store(s, g_ref[pl.ds(off, two_l), :], DFT_L)
        return carry

    lax.fori_loop(0, nb, fwd2, 0, unroll=LOOP_UNROLL)

    def inv1(k1, carry):
        off = pl.multiple_of(k1 * two_l, two_l)
        out = jnp.dot(t2i_ref[...], p_ref[pl.ds(off, two_l), :], preferred_element_type=F32)
        a_ref[pl.ds(pl.multiple_of(k1 * p2, SUBLANES), two_l), :] = out
        return carry

    lax.fori_loop(0, nb, inv1, 0, unroll=LOOP_UNROLL)

    def inv2(n2, carry):
        rhs = jnp.concatenate([a_ref[pl.ds(n2, nb, stride=p2), :],
                               a_ref[pl.ds(DFT_L + n2, nb, stride=p2), :]], axis=0).astype(BF16)
        out = jnp.dot(t3_ref[n2], rhs, preferred_element_type=F32)
        y_ref[pl.ds(pl.multiple_of(n2 * p3, SUBLANES), nb), :] = out
        return carry

    lax.fori_loop(0, DFT_L, inv2, 0, unroll=SMALL_LOOP_UNROLL)

    skip = skip_ref[...]

    def epilogue(n1, carry):
        off = pl.multiple_of(n1 * DFT_L, DFT_L)
        for b, vs_ref in enumerate((va_ref, vb_ref)):
            y = y_ref[pl.ds(b * half + n1, DFT_L, stride=p3), :]
            gate = gate_ref[b, pl.ds(off, DFT_L), :].astype(F32)
            o_ref[b, pl.ds(off, DFT_L), :] = (gate * (y + skip * vs_ref[pl.ds(off, DFT_L), :])).astype(o_ref.dtype)
        return carry

    lax.fori_loop(0, half, epilogue, 0, unroll=LOOP_UNROLL)


def _pitch(rows):
    p = -(-rows // SUBLANES)
    if p % 2 == 0:
        p += 1
    return p * SUBLANES


def _conv_specs(n, gate_blk, v_blk, spec_rows, spec_blk):
    return [
        pl.BlockSpec((2, n, LANES), lambda j, p: (p, 0, gate_blk + j)),
        pl.BlockSpec((2, n, LANES), lambda j, p: (p, 0, v_blk + j)),
        pl.BlockSpec((1, LANES), lambda j, p: (0, j)),
        pl.BlockSpec((spec_rows, LANES), lambda j, p: (0, spec_blk + j), pipeline_mode=pl.Buffered(1)),
    ]


def _hyena_conv(gate, gate_blk, v, v_blk, skip, spec, spec_blk, tables):
    bsz, n, _ = v.shape
    cw = skip.shape[-1]
    out_shape = jax.ShapeDtypeStruct((bsz, n, cw), BF16)
    out_spec = pl.BlockSpec((2, n, LANES), lambda j, p: (p, 0, j))
    grid = (cw // LANES, bsz // 2)
    params = _cparams(("parallel", "arbitrary"))
    if len(tables) == 2:
        fwd, inv = tables
        return pl.pallas_call(
            _hyena_conv_small_kernel,
            name="long_conv_small",
            grid=grid,
            in_specs=_conv_specs(n, gate_blk, v_blk, 4 * n, spec_blk) + [_const_spec(fwd.shape), _const_spec(inv.shape)],
            out_specs=out_spec,
            out_shape=out_shape,
            compiler_params=params,
        )(gate, v, skip, spec, fwd, inv)
    nb = 2 * n // DFT_L
    t1, t2, t2i, t3 = tables
    p2, p3 = _pitch(2 * DFT_L), _pitch(nb)
    return pl.pallas_call(
        functools.partial(_hyena_conv_kernel, nb=nb, p2=p2, p3=p3),
        name="long_conv",
        grid=grid,
        in_specs=_conv_specs(n, gate_blk, v_blk, 4 * n, spec_blk) + [_const_spec(t.shape) for t in tables],
        out_specs=out_spec,
        out_shape=out_shape,
        scratch_shapes=[
            pltpu.VMEM((n, LANES), F32), pltpu.VMEM((n, LANES), F32),
            pltpu.VMEM((max(DFT_L * P1, nb * p2), LANES), F32),
            pltpu.VMEM((4 * n, LANES), BF16),
            pltpu.VMEM((DFT_L * p3, LANES), F32),
        ],
        compiler_params=params,
    )(gate, v, skip, spec, t1, t2, t2i, t3)


def _small_tables(n):
    big = 2 * n
    k = jnp.arange(big, dtype=jnp.int32)
    fr, fi = _cis(k[:, None] * k[None, :], big, -1.0)
    fwd = _cblock(fr[:, :n], fi[:, :n])
    real = jnp.concatenate([fr, fi], axis=0)
    inv = _cblock(fr[:n, :], -fi[:n, :]) / big
    return fwd.astype(BF16), real.astype(BF16), inv.astype(BF16)


def _spectrum_small_kernel(xa_ref, xb_ref, f_ref, o_ref):
    taps = jnp.concatenate([xa_ref[...], xb_ref[...]], axis=0).astype(BF16)
    o_ref[...] = jnp.dot(f_ref[...], taps, preferred_element_type=F32)


def _filter_spectrum_small(first, second, real):
    n, cols = first.shape
    col = lambda j: (0, j)
    return pl.pallas_call(
        _spectrum_small_kernel,
        name="filter_spectrum_small",
        grid=(cols // LANES,),
        in_specs=[pl.BlockSpec((n, LANES), col), pl.BlockSpec((n, LANES), col), _const_spec(real.shape)],
        out_specs=pl.BlockSpec((4 * n, LANES), col),
        out_shape=jax.ShapeDtypeStruct((4 * n, cols), F32),
        compiler_params=_cparams(("parallel",)),
    )(first, second, real)


def _hyena_conv_small_kernel(gate_ref, v_ref, skip_ref, g_ref, f_ref, i_ref, o_ref):
    n = v_ref.shape[1]
    rhs = jnp.concatenate([v_ref[0], v_ref[1]], axis=0)
    s = jnp.dot(f_ref[...], rhs, preferred_element_type=F32)
    y = jnp.dot(i_ref[...], _cmul_store(s, g_ref[...], 2 * n), preferred_element_type=F32)
    skip = skip_ref[...]
    for b in range(2):
        yb = y[b * n:(b + 1) * n] + skip * v_ref[b].astype(F32)
        o_ref[b] = (gate_ref[b].astype(F32) * yb).astype(o_ref.dtype)


def _hyena(u, u_blk, conv_w, conv_b, hy_b, spec, tables):
    cw = hy_b.shape[-1]
    cb = cw // LANES
    uc = _short_conv(u, u_blk, conv_w, conv_b)
    z = _hyena_conv(uc, 0, uc, 2 * cb, hy_b[0:1], spec, 0, tables)
    return _hyena_conv(uc, cb, z, 0, hy_b[1:2], spec, cb, tables)


def kernel(x, c, ctx, c_ctx, mod_w, mod_b, norm_w, ffn_w1, ffn_w3, ffn_w2, ab_w_in, ab_w_out, diff_lambda,
           diff_subln_w, hy_conv_w, hy_conv_b, hy_f_w0, hy_f_b0, hy_f_w1, hy_f_b1, hy_f_freq, hy_f_wout,
           hy_bias, na_w_in, na_w_out, na_rpb, final_norm_w):
    bsz, n, d = x.shape
    n_ctx = ctx.shape[1]
    depth = mod_w.shape[0]
    a_width = A_HEADS * A_V_DIM
    b_width = hy_bias.shape[-1]
    c_width = C_HEADS * C_HEAD_DIM
    qk_scale = A_QK_DIM ** -0.5 * LOG2E
    na_scale = C_HEAD_DIM ** -0.5 * LOG2E

    mod = _modulation(c, c_ctx, mod_w, mod_b)
    lat_row = lambda b: b
    ctx_row = lambda b: bsz
    w1, w3, w2 = ffn_w1.astype(BF16), ffn_w3.astype(BF16), ffn_w2.astype(BF16)

    lat, cx = x, ctx
    for layer in range(depth):
        need_ctx = layer < depth - 1
        last = layer == depth - 1
        m, g = mod[layer], norm_w[layer]
        i = layer // 2
        lat = _ffn(lat, m, lat_row, g, w1, w3, w2, (layer, 0), 0)
        cx = _ffn(cx, m, ctx_row, g, w1, w3, w2, (layer, 0), 0)
        if layer % 2 == 0:
            w_in = ab_w_in[i].astype(BF16)
            w_out = ab_w_out[i].astype(BF16)
            lam_init = 0.8 - 0.6 * math.exp(-0.3 * layer)
            seg = [(a_width, qk_scale, True, 0), (a_width, 1.0, True, 0), (a_width, 1.0, False, 0),
                   (3 * b_width, 1.0, False, 0)]
            q_l, k_l, v_l, u_l = _project(lat, m, lat_row, g, w_in, seg, _rope_tables(n, a_width // A_QK_DIM))
            seg_c = [(s[0], s[1], False, s[3]) for s in seg]
            if need_ctx:
                q_c, k_c, v_c, u_c = _project(cx, m, ctx_row, g, w_in, seg_c)
            else:
                k_c, v_c = _project(cx, m, ctx_row, g, w_in[:, a_width:3 * a_width], seg_c[1:3])
            a_l = _diff_attention_rows(q_l, [k_c, k_l], [v_c, v_l], diff_lambda[i], diff_subln_w[i], lam_init)

            def hy(u, length):
                first, second = _hyena_filters(length, hy_f_w0[i], hy_f_b0[i], hy_f_w1[i], hy_f_b1[i],
                                               hy_f_freq[i], hy_f_wout[i], b_width)
                if 2 * length // DFT_L < 2 * SUBLANES:
                    fwd, real, inv = _small_tables(length)
                    spec = _filter_spectrum_small(first, second, real)
                    tables = (fwd, inv)
                else:
                    t1d, t1r, t2, t2i, t3 = _dft_tables(length)
                    spec = _filter_spectrum(first, second, t1r, t2)
                    tables = (t1d, t2, t2i, t3)
                return _hyena(u, 0, hy_conv_w[i], hy_conv_b[i], hy_bias[i], spec, tables)

            b_l = hy(u_l, n)
            pre_l = ([a_l, b_l], w_out)
            if need_ctx:
                a_c = _diff_attention_rows(q_c, [k_c], [v_c], diff_lambda[i], diff_subln_w[i], lam_init)
                b_c = hy(u_c, n_ctx)
                pre_c = ([a_c, b_c], w_out)
        else:
            w_in = na_w_in[i].astype(BF16)
            w_out = na_w_out[i].astype(BF16)
            seg = [(c_width, na_scale, False, ATTN_CHUNK), (c_width, 1.0, False, 0), (c_width, 1.0, False, 2 * GRID_W)]
            q_l, k_l, v_l = _project(lat, m, lat_row, g, w_in, seg)
            if need_ctx:
                raise NotImplementedError("context outputs of the neighbourhood mixer are not needed at this depth")
            k_c, v_c = _project(cx, m, ctx_row, g, w_in[:, c_width:], seg[1:])
            o_l = _neighbourhood_attention(q_l, k_l, v_l, k_c, v_c, na_rpb[i])
            pre_l = ([o_l], w_out)
        lat = _ffn(lat, m, lat_row, g, w1, w3, w2, (layer, 1), 2, pre=pre_l,
                   final_w=final_norm_w if last else None)
        if need_ctx:
            cx = _ffn(cx, m, ctx_row, g, w1, w3, w2, (layer, 1), 2, pre=pre_c)
    return lat
```

```python
import functools
import math

import jax
import jax.numpy as jnp
from jax import lax
from jax.experimental import pallas as pl
from jax.experimental.pallas import tpu as pltpu

F32 = jnp.float32
BF16 = jnp.bfloat16
HIGHEST = lax.Precision.HIGHEST

GRID_W = 64
N_MOD = 9
NORM_EPS = 1e-6
A_HEADS = 4
A_QK_DIM = 64
A_V_DIM = 128
HY_EMB = 33
HY_BANDS = 16
HY_TARGET = 1e-2
HY_MAX_DECAY = math.log(HY_TARGET) / 0.3
HY_MIN_DECAY = math.log(HY_TARGET) / 1.5
C_HEADS = 16
C_HEAD_DIM = 64
NA_ROWS = 8
NA_COLS = 16
ROPE_BASE = 10000.0

LANES = 128
SUBLANES = 8
VMEM_LIMIT = 56 * 2**20

NA_QROWS = 4
NA_KROWS = NA_QROWS + NA_ROWS
NEG = -1e30
ATTN_CHUNK = 512
SCORE_ROWS = 64
LOG2E = math.log2(math.e)

DFT_L = 128
P1 = DFT_L + 8
LOOP_UNROLL = 8
SMALL_LOOP_UNROLL = 16


def _cparams(sem):
    return pltpu.CompilerParams(dimension_semantics=sem, vmem_limit_bytes=VMEM_LIMIT)


def _const_spec(shape):
    nd = len(shape)
    return pl.BlockSpec(shape, lambda *_: (0,) * nd, pipeline_mode=pl.Buffered(1))


def _rms(x, w):
    ms = jnp.mean(x * x, axis=-1, keepdims=True)
    return x * lax.rsqrt(ms + NORM_EPS) * w


def _sigmoid(x):
    return 1.0 / (1.0 + jnp.exp(-x))


def _mod_kernel(s_ref, w_ref, b_ref, o_ref):
    s = s_ref[...]
    s = s * _sigmoid(s)
    o_ref[0] = jnp.dot(s, w_ref[0], precision=HIGHEST, preferred_element_type=F32) + b_ref[0]


def _modulation(c, c_ctx, mod_w, mod_b):
    depth, d, nd = mod_w.shape
    bsz = c.shape[0]
    rows = -(-(bsz + 1) // SUBLANES) * SUBLANES
    s = jnp.zeros((rows, d), F32).at[:bsz].set(c).at[bsz].set(c_ctx)
    tn = 1024
    out = pl.pallas_call(
        _mod_kernel,
        name="modulation",
        grid=(depth, nd // tn),
        in_specs=[
            pl.BlockSpec((rows, d), lambda l, j: (0, 0)),
            pl.BlockSpec((1, d, tn), lambda l, j: (l, 0, j)),
            pl.BlockSpec((1, 1, tn), lambda l, j: (l, 0, j)),
        ],
        out_specs=pl.BlockSpec((1, rows, tn), lambda l, j: (l, 0, j)),
        out_shape=jax.ShapeDtypeStruct((depth, rows, nd), F32),
        compiler_params=_cparams(("arbitrary", "arbitrary")),
    )(s, mod_w, mod_b.reshape(depth, 1, nd))
    return out[:, :bsz + 1].reshape(depth, bsz + 1, N_MOD, d)


def _ffn_kernel(*refs, k, pre_widths, final):
    it = iter(refs)
    x_ref, mod_ref, g_ref, w1_ref, w3_ref, w2_ref = (next(it) for _ in range(6))
    y_refs = [next(it) for _ in pre_widths]
    wo_ref = next(it) if pre_widths else None
    fw_ref = next(it) if final else None
    o_ref = next(it)

    x = x_ref[0]
    if pre_widths:
        off = 0
        y = None
        for y_ref, w in zip(y_refs, pre_widths):
            t = jnp.dot(y_ref[0], wo_ref[off:off + w, :], preferred_element_type=F32)
            y = t if y is None else y + t
            off += w
        x = x + mod_ref[0, 5:6, :] * y
    shift = mod_ref[0, 3 * k:3 * k + 1, :]
    scale = mod_ref[0, 3 * k + 1:3 * k + 2, :]
    gate = mod_ref[0, 3 * k + 2:3 * k + 3, :]
    hn = (_rms(x, g_ref[k:k + 1, :]) * (1.0 + scale) + shift).astype(BF16)
    a = jnp.dot(hn, w1_ref[...], preferred_element_type=F32)
    b = jnp.dot(hn, w3_ref[...], preferred_element_type=F32)
    act = (a * _sigmoid(a) * b).astype(BF16)
    out = x + 0.5 * gate * jnp.dot(act, w2_ref[...], preferred_element_type=F32)
    if final:
        out = _rms(out, fw_ref[...])
    o_ref[0] = out


def _ffn(x, mod, mod_row, g, w1, w3, w2, widx, k, pre=None, final_w=None):
    bsz, n, d = x.shape
    d_ff = w1.shape[-1]
    tm = min(n, 512)
    row = lambda b, i: (b, i, 0)
    pick = lambda b, i: (widx[0], widx[1], 0, 0)
    in_specs = [
        pl.BlockSpec((1, tm, d), row),
        pl.BlockSpec((1, N_MOD, d), lambda b, i: (mod_row(b), 0, 0)),
        _const_spec(g.shape),
        pl.BlockSpec((None, None, d, d_ff), pick, pipeline_mode=pl.Buffered(1)),
        pl.BlockSpec((None, None, d, d_ff), pick, pipeline_mode=pl.Buffered(1)),
        pl.BlockSpec((None, None, d_ff, d), pick, pipeline_mode=pl.Buffered(1)),
    ]
    args = [x, mod, g, w1, w3, w2]
    pre_widths = ()
    if pre is not None:
        ys, wo = pre
        pre_widths = tuple(y.shape[-1] for y in ys)
        for y in ys:
            in_specs.append(pl.BlockSpec((1, tm, y.shape[-1]), row))
            args.append(y)
        in_specs.append(_const_spec(wo.shape))
        args.append(wo)
    if final_w is not None:
        in_specs.append(_const_spec((1, d)))
        args.append(final_w.reshape(1, d))
    return pl.pallas_call(
        functools.partial(_ffn_kernel, k=k, pre_widths=pre_widths, final=final_w is not None),
        name="macaron_ffn",
        grid=(bsz, n // tm),
        in_specs=in_specs,
        out_specs=pl.BlockSpec((1, tm, d), row),
        out_shape=jax.ShapeDtypeStruct((bsz, n, d), F32),
        compiler_params=_cparams(("parallel", "parallel")),
    )(*args)


def _rope(u, cos, sin_signed, axis):
    w = u.shape[axis]
    idx = lax.broadcasted_iota(jnp.int32, u.shape, axis)
    partner = jnp.where(idx % 32 < 16, pltpu.roll(u, w - 16, axis), pltpu.roll(u, 16, axis))
    return u * cos + partner * sin_signed


def _proj_kernel(*refs, segments, use_rope, any_major):
    it = iter(refs)
    x_ref, mod_ref, g_ref, w_ref = (next(it) for _ in range(4))
    wt_ref = next(it) if any_major else None
    cos_ref, sin_ref, cost_ref, sint_ref = (next(it) if use_rope else None for _ in range(4))
    out_refs = list(it)
    x = x_ref[0]
    hn = (_rms(x, g_ref[1:2, :]) * (1.0 + mod_ref[0, 4:5, :]) + mod_ref[0, 3:4, :]).astype(BF16)
    off = 0
    toff = 0
    for o_ref, (width, scale, rope, tchunk) in zip(out_refs, segments):
        if tchunk:
            ut = lax.dot_general(wt_ref[toff:toff + width, :], hn, (((1,), (1,)), ((), ())),
                                 preferred_element_type=F32)
            if rope:
                ut = _rope(ut, cost_ref[...], sint_ref[...], 0)
            if scale != 1.0:
                ut = ut * scale
            ut = ut.astype(o_ref.dtype)
            for j in range(ut.shape[1] // tchunk):
                o_ref[0, j] = ut[:, j * tchunk:(j + 1) * tchunk]
            toff += width
        else:
            u = jnp.dot(hn, w_ref[:, off:off + width], preferred_element_type=F32)
            if rope:
                u = _rope(u, cos_ref[...], sin_ref[...], 1)
            if scale != 1.0:
                u = u * scale
            o_ref[0] = u.astype(o_ref.dtype)
        off += width


def _project(x, mod, mod_row, g, w, segments, rope_tables=None):
    bsz, n, d = x.shape
    tm = min(n, 512)
    row = lambda b, i: (b, i, 0)
    out_specs, out_shape, wt_parts = [], [], []
    off = 0
    for width, _, _, tchunk in segments:
        if tchunk:
            tchunk = min(tchunk, tm)
            out_specs.append(pl.BlockSpec((1, tm // tchunk, width, tchunk), lambda b, i: (b, i, 0, 0)))
            out_shape.append(jax.ShapeDtypeStruct((bsz, n // tchunk, width, tchunk), BF16))
            wt_parts.append(w[:, off:off + width].T)
        else:
            out_specs.append(pl.BlockSpec((1, tm, width), row))
            out_shape.append(jax.ShapeDtypeStruct((bsz, n, width), BF16))
        off += width
    segments = tuple((w_, s_, r_, min(t_, tm)) for w_, s_, r_, t_ in segments)
    in_specs = [
        pl.BlockSpec((1, tm, d), row),
        pl.BlockSpec((1, N_MOD, d), lambda b, i: (mod_row(b), 0, 0)),
        _const_spec(g.shape),
        _const_spec(w.shape),
    ]
    args = [x, mod, g, w]
    if wt_parts:
        wt = jnp.concatenate(wt_parts, axis=0)
        in_specs.append(_const_spec(wt.shape))
        args.append(wt)
    if rope_tables is not None:
        for t in rope_tables:
            in_specs.append(pl.BlockSpec((tm, t.shape[1]), lambda b, i: (i, 0)))
            args.append(t)
        for t in rope_tables:
            in_specs.append(pl.BlockSpec((t.shape[1], tm), lambda b, i: (0, i)))
            args.append(t.T)
    return pl.pallas_call(
        functools.partial(_proj_kernel, segments=segments, use_rope=rope_tables is not None,
                          any_major=bool(wt_parts)),
        name="mixer_projection",
        grid=(bsz, n // tm),
        in_specs=in_specs,
        out_specs=out_specs,
        out_shape=out_shape,
        compiler_params=_cparams(("parallel", "parallel")),
    )(*args)


def _rope_tables(n, reps):
    t = jnp.arange(n)
    nf = A_QK_DIM // 4
    inv_freq = ROPE_BASE ** (-jnp.arange(nf, dtype=F32) / nf)
    cols, sins = [], []
    for pos in (t // GRID_W, t % GRID_W):
        ang = pos.astype(F32)[:, None] * inv_freq
        c, s = jnp.cos(ang), jnp.sin(ang)
        cols += [c, c]
        sins += [-s, s]
    return jnp.tile(jnp.concatenate(cols, 1), (1, reps)), jnp.tile(jnp.concatenate(sins, 1), (1, reps))


def _exp_rows(s_ref, e_ref, rows, m):
    for r in range(0, rows, SCORE_ROWS):
        e_ref[r:r + SCORE_ROWS, :] = jnp.exp2(s_ref[r:r + SCORE_ROWS, :] - m).astype(BF16)


def _dattn_rows_kernel(*refs, n_src, lam_init):
    lam_ref, sw_ref, q_ref = refs[:3]
    k_refs = refs[3:3 + n_src]
    v_refs = refs[3 + n_src:3 + 2 * n_src]
    o_ref = refs[3 + 2 * n_src]
    q = q_ref[0]
    lane = lax.broadcasted_iota(jnp.int32, q.shape, 1)
    zero = jnp.zeros_like(q)
    nt = (((1,), (1,)), ((), ()))
    qms = (jnp.where(lane < A_QK_DIM, q, zero), jnp.where(lane >= A_QK_DIM, q, zero))
    scores = [[lax.dot_general(qm, k_ref[0], nt, preferred_element_type=F32) for k_ref in k_refs] for qm in qms]
    outs = []
    for ss in scores:
        m = None
        for s in ss:
            ms = jnp.max(s, axis=-1, keepdims=True)
            m = ms if m is None else jnp.maximum(m, ms)
        acc = None
        den = None
        for s, v_ref in zip(ss, v_refs):
            e = jnp.exp2(s - m)
            ds = jnp.sum(e, axis=-1, keepdims=True)
            t = jnp.dot(e.astype(BF16), v_ref[0], preferred_element_type=F32)
            den = ds if den is None else den + ds
            acc = t if acc is None else acc + t
        outs.append(acc / den)
    lp = lam_ref[...]
    lam = (jnp.exp(jnp.sum(lp[0:1] * lp[1:2], axis=-1, keepdims=True))
           - jnp.exp(jnp.sum(lp[2:3] * lp[3:4], axis=-1, keepdims=True)) + lam_init)
    o = outs[0] - lam * outs[1]
    o_ref[0] = (_rms(o, sw_ref[...]) * (1.0 - lam_init)).astype(o_ref.dtype)


def _diff_attention_rows(q, ks, vs, lam_p, subln_w, lam_init):
    bsz, nq, width = q.shape
    tq = min(nq, 512)
    hb = lambda b, h, i: (b, i, h)
    kv = lambda b, h, i: (b, 0, h)
    in_specs = [
        pl.BlockSpec(lam_p.shape, lambda b, h, i: (0, 0)),
        pl.BlockSpec((1, A_V_DIM), lambda b, h, i: (0, 0)),
        pl.BlockSpec((1, tq, LANES), hb),
    ]
    in_specs += [pl.BlockSpec((1, k.shape[1], LANES), kv) for k in ks]
    in_specs += [pl.BlockSpec((1, v.shape[1], LANES), kv) for v in vs]
    return pl.pallas_call(
        functools.partial(_dattn_rows_kernel, n_src=len(ks), lam_init=lam_init),
        name="diff_attention_rows",
        grid=(bsz, A_HEADS, nq // tq),
        in_specs=in_specs,
        out_specs=pl.BlockSpec((1, tq, LANES), hb),
        out_shape=jax.ShapeDtypeStruct((bsz, nq, width), BF16),
        compiler_params=_cparams(("parallel", "parallel", "parallel")),
    )(lam_p, subln_w.reshape(1, A_V_DIM), q, *ks, *vs)


def _na_kernel(qt_ref, k_ref, vt_ref, kc_ref, vct_ref, tb_ref, o_ref, s_ref, e_ref, *, rows):
    i = pl.program_id(2)
    r0 = i * NA_QROWS
    start = jnp.clip(r0 - NA_ROWS // 2, 0, rows - NA_KROWS)
    koff = pl.multiple_of(start * GRID_W, 2 * GRID_W)
    kw = k_ref[0, pl.ds(koff, NA_KROWS * GRID_W), :]
    c0 = start // 2
    vwt = jnp.concatenate([vt_ref[0, c0 + j] for j in range(NA_KROWS // 2)], axis=1)
    kc = kc_ref[0]
    vct = jnp.concatenate([vct_ref[0, j] for j in range(vct_ref.shape[1])], axis=1)
    qt = qt_ref[0, 0]
    row = lax.broadcasted_iota(jnp.int32, qt.shape, 0)
    zero = jnp.zeros_like(qt)
    lane_b = lax.broadcasted_iota(jnp.int32, (GRID_W, LANES), 1)
    n_lat, n_all = kw.shape[0], kw.shape[0] + kc.shape[0]
    vt_all = jnp.concatenate([vwt, vct], axis=1)
    ones = jnp.ones((2 * SUBLANES, n_all), BF16)
    qms = (jnp.where(row < C_HEAD_DIM, qt, zero), jnp.where(row >= C_HEAD_DIM, qt, zero))
    for hh, qm in enumerate(qms):
        s_ref[hh, 0:n_lat, :] = jnp.dot(kw, qm, preferred_element_type=F32)
        s_ref[hh, n_lat:n_all, :] = jnp.dot(kc, qm, preferred_element_type=F32)
    outs = []
    for hh in range(2):
        mx = jnp.full((1, qt.shape[1]), NEG, F32)
        for m in range(NA_KROWS):
            kr = start + m
            blocks = []
            for jp in range(NA_QROWS // 2):
                idx = []
                for j in (2 * jp, 2 * jp + 1):
                    r = r0 + j
                    rs = jnp.clip(r - NA_ROWS // 2, 0, rows - NA_ROWS)
                    valid = jnp.logical_and(kr >= rs, kr < rs + NA_ROWS)
                    idx.append(jnp.where(valid, kr - r + NA_ROWS - 1, 2 * NA_ROWS - 1))
                blocks.append(jnp.where(lane_b < GRID_W, tb_ref[hh, idx[0]], tb_ref[hh, idx[1]]))
            blk = s_ref[hh, m * GRID_W:(m + 1) * GRID_W, :] + jnp.concatenate(blocks, axis=1)
            s_ref[hh, m * GRID_W:(m + 1) * GRID_W, :] = blk
            mx = jnp.maximum(mx, jnp.max(blk, axis=0, keepdims=True))
        for r in range(n_lat, n_all, SCORE_ROWS):
            mx = jnp.maximum(mx, jnp.max(s_ref[hh, r:r + SCORE_ROWS, :], axis=0, keepdims=True))
        _exp_rows(s_ref.at[hh], e_ref.at[hh], n_all, mx)
        e = e_ref[hh]
        acc = jnp.dot(vt_all, e, preferred_element_type=F32)
        l = jnp.dot(ones, e, preferred_element_type=F32)
        outs.append(acc / l[0:1])
    o_ref[0] = jnp.where(row < C_HEAD_DIM, outs[0], outs[1]).T.astype(o_ref.dtype)


def _na_rows_kernel(q_ref, k_ref, v_ref, kc_ref, vc_ref, tb_ref, o_ref, *, rows):
    i = pl.program_id(2)
    r0 = i * NA_QROWS
    start = jnp.clip(r0 - NA_ROWS // 2, 0, rows - NA_KROWS)
    koff = pl.multiple_of(start * GRID_W, GRID_W)
    kw = k_ref[0, pl.ds(koff, NA_KROWS * GRID_W), :]
    vw = v_ref[0, pl.ds(koff, NA_KROWS * GRID_W), :]
    kc = kc_ref[0]
    vc = vc_ref[0]
    q = q_ref[0]
    lane = lax.broadcasted_iota(jnp.int32, q.shape, 1)
    zero = jnp.zeros_like(q)
    lane_b = lax.broadcasted_iota(jnp.int32, (GRID_W, LANES), 1)
    nt = (((1,), (1,)), ((), ()))
    qms = (jnp.where(lane < C_HEAD_DIM, q, zero), jnp.where(lane >= C_HEAD_DIM, q, zero))
    scores = [(lax.dot_general(qm, kw, nt, preferred_element_type=F32),
               lax.dot_general(qm, kc, nt, preferred_element_type=F32)) for qm in qms]
    outs = []
    for hh, (s_lat, s_ctx) in enumerate(scores):
        bias_rows = []
        for j in range(NA_QROWS):
            r = r0 + j
            rs = jnp.clip(r - NA_ROWS // 2, 0, rows - NA_ROWS)
            blocks = []
            for mp in range(NA_KROWS // 2):
                idx = []
                for m in (2 * mp, 2 * mp + 1):
                    kr = start + m
                    valid = jnp.logical_and(kr >= rs, kr < rs + NA_ROWS)
                    idx.append(jnp.where(valid, kr - r + NA_ROWS - 1, 2 * NA_ROWS - 1))
                blocks.append(jnp.where(lane_b < GRID_W, tb_ref[hh, idx[0]], tb_ref[hh, idx[1]]))
            bias_rows.append(jnp.concatenate(blocks, axis=1))
        s_lat = s_lat + jnp.concatenate(bias_rows, axis=0)
        m = jnp.maximum(jnp.max(s_lat, axis=-1, keepdims=True), jnp.max(s_ctx, axis=-1, keepdims=True))
        e_lat = jnp.exp2(s_lat - m)
        e_ctx = jnp.exp2(s_ctx - m)
        denom = jnp.sum(e_lat, axis=-1, keepdims=True) + jnp.sum(e_ctx, axis=-1, keepdims=True)
        o = (jnp.dot(e_lat.astype(BF16), vw, preferred_element_type=F32)
             + jnp.dot(e_ctx.astype(BF16), vc, preferred_element_type=F32))
        outs.append(o / denom)
    o_ref[0] = jnp.where(lane < C_HEAD_DIM, outs[0], outs[1]).astype(o_ref.dtype)


def _neighbourhood_attention_rows(q, k, v, k_ctx, v_ctx, rpb):
    bsz, n, width = q.shape
    rows = n // GRID_W
    assert rows >= NA_KROWS and rows % NA_QROWS == 0
    n_ctx = k_ctx.shape[1]
    tq = NA_QROWS * GRID_W
    tb = _na_bias_table(rpb, key_major=False)
    kv = lambda b, h, i: (b, 0, h)
    return pl.pallas_call(
        functools.partial(_na_rows_kernel, rows=rows),
        name="neighbourhood_attention_rows",
        grid=(bsz, C_HEADS // 2, n // tq),
        in_specs=[
            pl.BlockSpec((1, tq, LANES), lambda b, h, i: (b, i, h)),
            pl.BlockSpec((1, n, LANES), kv),
            pl.BlockSpec((1, n, LANES), kv),
            pl.BlockSpec((1, n_ctx, LANES), kv),
            pl.BlockSpec((1, n_ctx, LANES), kv),
            pl.BlockSpec((2, 2 * NA_ROWS, GRID_W, LANES), lambda b, h, i: (h, 0, 0, 0)),
        ],
        out_specs=pl.BlockSpec((1, tq, LANES), lambda b, h, i: (b, i, h)),
        out_shape=jax.ShapeDtypeStruct((bsz, n, width), BF16),
        compiler_params=_cparams(("parallel", "parallel", "arbitrary")),
    )(q, k, v, k_ctx, v_ctx, tb)


def _na_bias_table(rpb, key_major=True):
    nh = rpb.shape[0]
    c = jnp.arange(GRID_W)
    cs = jnp.clip(c - NA_COLS // 2, 0, GRID_W - NA_COLS)
    kc = jnp.arange(GRID_W)
    inside = (kc[None, :] >= cs[:, None]) & (kc[None, :] < cs[:, None] + NA_COLS)
    off = jnp.clip(kc[None, :] - c[:, None] + NA_COLS - 1, 0, 2 * NA_COLS - 2)
    tb = jnp.where(inside[None, None], rpb[:, :, off] * LOG2E, NEG)
    tb = jnp.concatenate([tb, jnp.full((nh, 1, GRID_W, GRID_W), NEG, F32)], axis=1)
    if key_major:
        tb = jnp.swapaxes(tb, -1, -2)
    return jnp.concatenate([tb, tb], axis=-1)


def _neighbourhood_attention(qt, k, vt, k_ctx, vt_ctx, rpb):
    bsz, n, width = k.shape
    rows = n // GRID_W
    assert rows >= NA_KROWS and rows % NA_QROWS == 0 and NA_QROWS % 2 == 0 and NA_ROWS % 4 == 0
    n_ctx = k_ctx.shape[1]
    tq = NA_QROWS * GRID_W
    n_keys = NA_KROWS * GRID_W + n_ctx
    per_chunk = qt.shape[3] // tq
    assert vt.shape[3] == 2 * GRID_W and per_chunk * tq == qt.shape[3]
    tb = _na_bias_table(rpb)
    kv = lambda b, h, i: (b, 0, h)
    kvt = lambda b, h, i: (b, 0, h, 0)
    return pl.pallas_call(
        functools.partial(_na_kernel, rows=rows),
        name="neighbourhood_attention",
        grid=(bsz, C_HEADS // 2, n // tq),
        in_specs=[
            pl.BlockSpec((1, 1, LANES, tq), lambda b, h, i: (b, i // per_chunk, h, i % per_chunk)),
            pl.BlockSpec((1, n, LANES), kv),
            pl.BlockSpec((1, vt.shape[1], LANES, vt.shape[3]), kvt),
            pl.BlockSpec((1, n_ctx, LANES), kv),
            pl.BlockSpec((1, vt_ctx.shape[1], LANES, vt_ctx.shape[3]), kvt),
            pl.BlockSpec((2, 2 * NA_ROWS, GRID_W, LANES), lambda b, h, i: (h, 0, 0, 0)),
        ],
        out_specs=pl.BlockSpec((1, tq, LANES), lambda b, h, i: (b, i, h)),
        out_shape=jax.ShapeDtypeStruct((bsz, n, width), BF16),
        scratch_shapes=[pltpu.VMEM((2, n_keys, tq), F32), pltpu.VMEM((2, n_keys, tq), BF16)],
        compiler_params=_cparams(("parallel", "parallel", "arbitrary")),
    )(qt, k, vt, k_ctx, vt_ctx, tb)


def _filter_kernel(z_ref, zr_ref, w0_ref, b0_ref, w1_ref, b1_ref, fr_ref, wf_ref, wb_ref, dl_ref,
                   first_ref, second_ref, *, n_inner, cw):
    fr = fr_ref[...]
    tt = z_ref.shape[0]
    row = lax.broadcasted_iota(jnp.int32, (tt, cw), 0) + pl.program_id(0) * tt
    for pos_ref, wo_ref, o_ref, drop_row0 in ((z_ref, wf_ref, first_ref, False), (zr_ref, wb_ref, second_ref, True)):
        z = pos_ref[...]
        hid = jnp.sin(fr * (jnp.dot(z, w0_ref[...], precision=HIGHEST, preferred_element_type=F32) + b0_ref[...]))
        for i in range(n_inner):
            hid = jnp.sin(fr * (jnp.dot(hid, w1_ref[i], precision=HIGHEST, preferred_element_type=F32)
                                + b1_ref[i:i + 1, :]))
        filt = jnp.dot(hid, wo_ref[...], precision=HIGHEST, preferred_element_type=F32)
        window = jnp.exp(-z[:, 0:1] * dl_ref[...])
        if drop_row0:
            window = jnp.where(row == 0, 0.0, window)
        for j in range(filt.shape[1] // cw):
            o_ref[:, j * cw:(j + 1) * cw] = filt[:, j * cw:(j + 1) * cw] * window


def _filter_features(pos, n):
    t = (pos.astype(F32) / (n - 1))[:, None]
    w = (2.0 * math.pi / n) * pos.astype(F32)[:, None]
    bands = jnp.linspace(1e-4, HY_BANDS - 1, HY_BANDS, dtype=F32)[None, :]
    z = jnp.concatenate([t, jnp.cos(bands * w), -jnp.sin(bands * w)], axis=-1)
    return jnp.pad(z, ((0, 0), (0, LANES - HY_EMB)))


def _hyena_filters(n, f_w0, f_b0, f_w1, f_b1, f_freq, f_wout, cw):
    pos = jnp.arange(n, dtype=jnp.int32)
    z = _filter_features(pos, n)
    zr = _filter_features((n - pos) % n, n)
    w0 = jnp.pad(f_w0, ((0, LANES - HY_EMB), (0, 0)))
    deltas = jnp.abs(jnp.linspace(HY_MIN_DECAY, HY_MAX_DECAY, cw, dtype=F32))[None, :]
    hid = f_w0.shape[1]
    orders = f_wout.shape[1] // (2 * cw)
    wo = f_wout.reshape(hid, orders, 2, cw)
    w_fwd = wo[:, :, 0].reshape(hid, orders * cw)
    w_bwd = wo[:, :, 1].reshape(hid, orders * cw)
    tt = min(n, 512)
    full = lambda shape: pl.BlockSpec(shape, lambda i: (0,) * len(shape))
    out_spec = pl.BlockSpec((tt, orders * cw), lambda i: (i, 0))
    out_shape = jax.ShapeDtypeStruct((n, orders * cw), F32)
    return pl.pallas_call(
        functools.partial(_filter_kernel, n_inner=f_w1.shape[0], cw=cw),
        name="hyena_filters",
        grid=(n // tt,),
        in_specs=[
            pl.BlockSpec((tt, LANES), lambda i: (i, 0)),
            pl.BlockSpec((tt, LANES), lambda i: (i, 0)),
            full(w0.shape), full((1, hid)), full(f_w1.shape), full(f_b1.shape), full((1, hid)),
            full(w_fwd.shape), full(w_bwd.shape), full((1, cw)),
        ],
        out_specs=[out_spec, out_spec],
        out_shape=[out_shape, out_shape],
        compiler_params=_cparams(("parallel",)),
    )(z, zr, w0, f_b0.reshape(1, hid), f_w1, f_b1, f_freq.reshape(1, hid), w_fwd, w_bwd, deltas)


def _cis(phase, period, sign):
    ang = (2.0 * math.pi / period) * (phase % period).astype(F32)
    return jnp.cos(ang), sign * jnp.sin(ang)


def _cblock(mr, mi):
    return jnp.concatenate([jnp.concatenate([mr, -mi], -1), jnp.concatenate([mi, mr], -1)], -2)


def _dft_tables(n):
    big = 2 * n
    nb = big // DFT_L
    f = jnp.arange(DFT_L, dtype=jnp.int32)
    k1 = jnp.arange(nb, dtype=jnp.int32)
    blk = jnp.arange(nb, dtype=jnp.int32)
    ph1 = f[:, None, None] * k1[None, :, None] + DFT_L * blk[None, None, :] * k1[None, :, None]
    cr, ci = _cis(ph1, big, -1.0)
    t1_data = _cblock(cr[:, :, :nb // 2], ci[:, :, :nb // 2])
    t1_real = jnp.concatenate([cr, ci], axis=1)
    ph2 = f[:, None] * f[None, :]
    fr, fi = _cis(ph2, DFT_L, -1.0)
    t2 = _cblock(fr, fi)
    t2_inv = _cblock(fr, -fi)
    n1 = jnp.arange(nb // 2, dtype=jnp.int32)
    ph3 = DFT_L * n1[None, :, None] * k1[None, None, :] + f[:, None, None] * k1[None, None, :]
    dr, di = _cis(ph3, big, 1.0)
    t3 = _cblock(dr, di) / big
    return t1_data.astype(BF16), t1_real.astype(BF16), t2.astype(BF16), t2_inv.astype(BF16), t3.astype(BF16)


def _dft_stage1(xa_ref, xb_ref, t1_ref, a_ref, nb):
    half = nb // 2

    def body(f, carry):
        rhs = jnp.concatenate([xa_ref[pl.ds(f, half, stride=DFT_L), :],
                               xb_ref[pl.ds(f, half, stride=DFT_L), :]], axis=0).astype(BF16)
        out = jnp.dot(t1_ref[f], rhs, preferred_element_type=F32)
        a_ref[pl.ds(pl.multiple_of(f * P1, SUBLANES), 2 * nb), :] = out
        return carry

    lax.fori_loop(0, DFT_L, body, 0, unroll=SMALL_LOOP_UNROLL)


def _dft_stage2_in(a_ref, k1, nb):
    return jnp.concatenate([a_ref[pl.ds(k1, DFT_L, stride=P1), :],
                            a_ref[pl.ds(nb + k1, DFT_L, stride=P1), :]], axis=0).astype(BF16)


def _spectrum_kernel(xa_ref, xb_ref, t1_ref, t2_ref, o_ref, a_ref, *, nb):
    _dft_stage1(xa_ref, xb_ref, t1_ref, a_ref, nb)

    def body(k1, carry):
        out = jnp.dot(t2_ref[...], _dft_stage2_in(a_ref, k1, nb), preferred_element_type=F32)
        o_ref[pl.ds(pl.multiple_of(k1 * 2 * DFT_L, 2 * DFT_L), 2 * DFT_L), :] = out
        return carry

    lax.fori_loop(0, nb, body, 0, unroll=LOOP_UNROLL)


def _filter_spectrum(first, second, t1_real, t2):
    n, cols = first.shape
    nb = 2 * n // DFT_L
    col = lambda j: (0, j)
    return pl.pallas_call(
        functools.partial(_spectrum_kernel, nb=nb),
        name="filter_spectrum",
        grid=(cols // LANES,),
        in_specs=[
            pl.BlockSpec((n, LANES), col),
            pl.BlockSpec((n, LANES), col),
            _const_spec(t1_real.shape),
            _const_spec(t2.shape),
        ],
        out_specs=pl.BlockSpec((4 * n, LANES), col),
        out_shape=jax.ShapeDtypeStruct((4 * n, cols), F32),
        scratch_shapes=[pltpu.VMEM((DFT_L * P1, LANES), F32)],
        compiler_params=_cparams(("parallel",)),
    )(first, second, t1_real, t2)


def _short_conv_kernel(u_ref, w_ref, b_ref, o_ref):
    u = u_ref[0].astype(F32)
    n = u.shape[0]
    w0, w1, w2 = w_ref[0:1, :], w_ref[1:2, :], w_ref[2:3, :]
    out = pltpu.roll(u, 1, 0) * w0 + u * w1 + pltpu.roll(u, n - 1, 0) * w2 + b_ref[...]
    row = lax.broadcasted_iota(jnp.int32, (SUBLANES, u.shape[1]), 0)
    head = out[0:SUBLANES] - jnp.where(row == 0, u[n - 1:n] * w0, 0.0)
    tail = out[n - SUBLANES:n] - jnp.where(row == SUBLANES - 1, u[0:1] * w2, 0.0)
    o_ref[0] = jnp.concatenate([head, out[SUBLANES:n - SUBLANES], tail], axis=0).astype(o_ref.dtype)


def _short_conv(u, u_blk, w, b):
    bsz, n, _ = u.shape
    taps, ch = w.shape
    cb = 2 * LANES
    assert ch % cb == 0 and (u_blk * LANES) % cb == 0
    first = u_blk * LANES // cb
    return pl.pallas_call(
        _short_conv_kernel,
        name="short_conv",
        grid=(bsz, ch // cb),
        in_specs=[
            pl.BlockSpec((1, n, cb), lambda bi, j: (bi, 0, first + j)),
            pl.BlockSpec((taps, cb), lambda bi, j: (0, j)),
            pl.BlockSpec((1, cb), lambda bi, j: (0, j)),
        ],
        out_specs=pl.BlockSpec((1, n, cb), lambda bi, j: (bi, 0, j)),
        out_shape=jax.ShapeDtypeStruct((bsz, n, ch), BF16),
        compiler_params=_cparams(("parallel", "parallel")),
    )(u, w, b.reshape(1, ch))


def _cmul_store(s, g, rows):
    sr, si = s[:rows], s[rows:]
    gr, gi = g[:rows], g[rows:]
    return jnp.concatenate([sr * gr - si * gi, sr * gi + si * gr], 0).astype(BF16)


def _hyena_conv_kernel(gate_ref, v_ref, skip_ref, g_ref, t1_ref, t2_ref, t2i_ref, t3_ref, o_ref,
                       va_ref, vb_ref, a_ref, p_ref, y_ref, *, nb, p2, p3):
    half = nb // 2
    two_l = 2 * DFT_L
    va_ref[...] = v_ref[0].astype(F32)
    vb_ref[...] = v_ref[1].astype(F32)

    _dft_stage1(va_ref, vb_ref, t1_ref, a_ref, nb)

    def fwd2(k1, carry):
        s = jnp.dot(t2_ref[...], _dft_stage2_in(a_ref, k1, nb), preferred_element_type=F32)
        off = pl.multiple_of(k1 * two_l, two_l)
        p_ref[pl.ds(off, two_l), :] = _cmul_store(s, g_ref[pl.ds(off, two_l), :], DFT_L)
        return carry

    lax.fori_loop(0, nb, fwd2, 0, unroll=LOOP_UNROLL)

    def inv1(k1, carry):
        off = pl.multiple_of(k1 * two_l, two_l)
        out = jnp.dot(t2i_ref[...], p_ref[pl.ds(off, two_l), :], preferred_element_type=F32)
        a_ref[pl.ds(pl.multiple_of(k1 * p2, SUBLANES), two_l), :] = out
        return carry

    lax.fori_loop(0, nb, inv1, 0, unroll=LOOP_UNROLL)

    def inv2(n2, carry):
        rhs = jnp.concatenate([a_ref[pl.ds(n2, nb, stride=p2), :],
                               a_ref[pl.ds(DFT_L + n2, nb, stride=p2), :]], axis=0).astype(BF16)
        out = jnp.dot(t3_ref[n2], rhs, preferred_element_type=F32)
        y_ref[pl.ds(pl.multiple_of(n2 * p3, SUBLANES), nb), :] = out
        return carry

    lax.fori_loop(0, DFT_L, inv2, 0, unroll=SMALL_LOOP_UNROLL)

    skip = skip_ref[...]

    def epilogue(n1, carry):
        off = pl.multiple_of(n1 * DFT_L, DFT_L)
        for b, vs_ref in enumerate((va_ref, vb_ref)):
            y = y_ref[pl.ds(b * half + n1, DFT_L, stride=p3), :]
            gate = gate_ref[b, pl.ds(off, DFT_L), :].astype(F32)
            o_ref[b, pl.ds(off, DFT_L), :] = (gate * (y + skip * vs_ref[pl.ds(off, DFT_L), :])).astype(o_ref.dtype)
        return carry

    lax.fori_loop(0, half, epilogue, 0, unroll=LOOP_UNROLL)


def _pitch(rows):
    p = -(-rows // SUBLANES)
    if p % 2 == 0:
        p += 1
    return p * SUBLANES


def _conv_specs(n, gate_blk, v_blk, spec_rows, spec_blk):
    return [
        pl.BlockSpec((2, n, LANES), lambda j, p: (p, 0, gate_blk + j)),
        pl.BlockSpec((2, n, LANES), lambda j, p: (p, 0, v_blk + j)),
        pl.BlockSpec((1, LANES), lambda j, p: (0, j)),
        pl.BlockSpec((spec_rows, LANES), lambda j, p: (0, spec_blk + j), pipeline_mode=pl.Buffered(1)),
    ]


def _hyena_conv(gate, gate_blk, v, v_blk, skip, spec, spec_blk, tables):
    bsz, n, _ = v.shape
    cw = skip.shape[-1]
    out_shape = jax.ShapeDtypeStruct((bsz, n, cw), BF16)
    out_spec = pl.BlockSpec((2, n, LANES), lambda j, p: (p, 0, j))
    grid = (cw // LANES, bsz // 2)
    params = _cparams(("parallel", "arbitrary"))
    if len(tables) == 2:
        fwd, inv = tables
        return pl.pallas_call(
            _hyena_conv_small_kernel,
            name="long_conv_small",
            grid=grid,
            in_specs=_conv_specs(n, gate_blk, v_blk, 4 * n, spec_blk) + [_const_spec(fwd.shape), _const_spec(inv.shape)],
            out_specs=out_spec,
            out_shape=out_shape,
            compiler_params=params,
        )(gate, v, skip, spec, fwd, inv)
    nb = 2 * n // DFT_L
    t1, t2, t2i, t3 = tables
    p2, p3 = _pitch(2 * DFT_L), _pitch(nb)
    return pl.pallas_call(
        functools.partial(_hyena_conv_kernel, nb=nb, p2=p2, p3=p3),
        name="long_conv",
        grid=grid,
        in_specs=_conv_specs(n, gate_blk, v_blk, 4 * n, spec_blk) + [_const_spec(t.shape) for t in tables],
        out_specs=out_spec,
        out_shape=out_shape,
        scratch_shapes=[
            pltpu.VMEM((n, LANES), F32), pltpu.VMEM((n, LANES), F32),
            pltpu.VMEM((max(DFT_L * P1, nb * p2), LANES), F32),
            pltpu.VMEM((4 * n, LANES), BF16),
            pltpu.VMEM((DFT_L * p3, LANES), F32),
        ],
        compiler_params=params,
    )(gate, v, skip, spec, t1, t2, t2i, t3)


def _small_tables(n):
    big = 2 * n
    k = jnp.arange(big, dtype=jnp.int32)
    fr, fi = _cis(k[:, None] * k[None, :], big, -1.0)
    fwd = _cblock(fr[:, :n], fi[:, :n])
    real = jnp.concatenate([fr, fi], axis=0)
    inv = _cblock(fr[:n, :], -fi[:n, :]) / big
    return fwd.astype(BF16), real.astype(BF16), inv.astype(BF16)


def _spectrum_small_kernel(xa_ref, xb_ref, f_ref, o_ref):
    taps = jnp.concatenate([xa_ref[...], xb_ref[...]], axis=0).astype(BF16)
    o_ref[...] = jnp.dot(f_ref[...], taps, preferred_element_type=F32)


def _filter_spectrum_small(first, second, real):
    n, cols = first.shape
    col = lambda j: (0, j)
    return pl.pallas_call(
        _spectrum_small_kernel,
        name="filter_spectrum_small",
        grid=(cols // LANES,),
        in_specs=[pl.BlockSpec((n, LANES), col), pl.BlockSpec((n, LANES), col), _const_spec(real.shape)],
        out_specs=pl.BlockSpec((4 * n, LANES), col),
        out_shape=jax.ShapeDtypeStruct((4 * n, cols), F32),
        compiler_params=_cparams(("parallel",)),
    )(first, second, real)


def _hyena_conv_small_kernel(gate_ref, v_ref, skip_ref, g_ref, f_ref, i_ref, o_ref):
    n = v_ref.shape[1]
    rhs = jnp.concatenate([v_ref[0], v_ref[1]], axis=0)
    s = jnp.dot(f_ref[...], rhs, preferred_element_type=F32)
    y = jnp.dot(i_ref[...], _cmul_store(s, g_ref[...], 2 * n), preferred_element_type=F32)
    skip = skip_ref[...]
    for b in range(2):
        yb = y[b * n:(b + 1) * n] + skip * v_ref[b].astype(F32)
        o_ref[b] = (gate_ref[b].astype(F32) * yb).astype(o_ref.dtype)


def _hyena(u, u_blk, conv_w, conv_b, hy_b, spec, tables):
    cw = hy_b.shape[-1]
    cb = cw // LANES
    uc = _short_conv(u, u_blk, conv_w, conv_b)
    z = _hyena_conv(uc, 0, uc, 2 * cb, hy_b[0:1], spec, 0, tables)
    return _hyena_conv(uc, cb, z, 0, hy_b[1:2], spec, cb, tables)


def kernel(x, c, ctx, c_ctx, mod_w, mod_b, norm_w, ffn_w1, ffn_w3, ffn_w2, ab_w_in, ab_w_out, diff_lambda,
           diff_subln_w, hy_conv_w, hy_conv_b, hy_f_w0, hy_f_b0, hy_f_w1, hy_f_b1, hy_f_freq, hy_f_wout,
           hy_bias, na_w_in, na_w_out, na_rpb, final_norm_w):
    bsz, n, d = x.shape
    n_ctx = ctx.shape[1]
    depth = mod_w.shape[0]
    a_width = A_HEADS * A_V_DIM
    b_width = hy_bias.shape[-1]
    c_width = C_HEADS * C_HEAD_DIM
    qk_scale = A_QK_DIM ** -0.5 * LOG2E
    na_scale = C_HEAD_DIM ** -0.5 * LOG2E

    mod = _modulation(c, c_ctx, mod_w, mod_b)
    lat_row = lambda b: b
    ctx_row = lambda b: bsz
    w1, w3, w2 = ffn_w1.astype(BF16), ffn_w3.astype(BF16), ffn_w2.astype(BF16)

    lat, cx = x, ctx
    for layer in range(depth):
        need_ctx = layer < depth - 1
        last = layer == depth - 1
        m, g = mod[layer], norm_w[layer]
        i = layer // 2
        lat = _ffn(lat, m, lat_row, g, w1, w3, w2, (layer, 0), 0)
        cx = _ffn(cx, m, ctx_row, g, w1, w3, w2, (layer, 0), 0)
        if layer % 2 == 0:
            w_in = ab_w_in[i].astype(BF16)
            w_out = ab_w_out[i].astype(BF16)
            lam_init = 0.8 - 0.6 * math.exp(-0.3 * layer)
            seg = [(a_width, qk_scale, True, 0), (a_width, 1.0, True, 0), (a_width, 1.0, False, 0),
                   (3 * b_width, 1.0, False, 0)]
            q_l, k_l, v_l, u_l = _project(lat, m, lat_row, g, w_in, seg, _rope_tables(n, a_width // A_QK_DIM))
            seg_c = [(s[0], s[1], False, s[3]) for s in seg]
            if need_ctx:
                q_c, k_c, v_c, u_c = _project(cx, m, ctx_row, g, w_in, seg_c)
            else:
                k_c, v_c = _project(cx, m, ctx_row, g, w_in[:, a_width:3 * a_width], seg_c[1:3])
            a_l = _diff_attention_rows(q_l, [k_c, k_l], [v_c, v_l], diff_lambda[i], diff_subln_w[i], lam_init)

            def hy(u, length):
                first, second = _hyena_filters(length, hy_f_w0[i], hy_f_b0[i], hy_f_w1[i], hy_f_b1[i],
                                               hy_f_freq[i], hy_f_wout[i], b_width)
                if 2 * length // DFT_L < 2 * SUBLANES:
                    fwd, real, inv = _small_tables(length)
                    spec = _filter_spectrum_small(first, second, real)
                    tables = (fwd, inv)
                else:
                    t1d, t1r, t2, t2i, t3 = _dft_tables(length)
                    spec = _filter_spectrum(first, second, t1r, t2)
                    tables = (t1d, t2, t2i, t3)
                return _hyena(u, 0, hy_conv_w[i], hy_conv_b[i], hy_bias[i], spec, tables)

            b_l = hy(u_l, n)
            pre_l = ([a_l, b_l], w_out)
            if need_ctx:
                a_c = _diff_attention_rows(q_c, [k_c], [v_c], diff_lambda[i], diff_subln_w[i], lam_init)
                b_c = hy(u_c, n_ctx)
                pre_c = ([a_c, b_c], w_out)
        else:
            w_in = na_w_in[i].astype(BF16)
            w_out = na_w_out[i].astype(BF16)
            seg = [(c_width, na_scale, False, 0), (c_width, 1.0, False, 0), (c_width, 1.0, False, 0)]
            q_l, k_l, v_l = _project(lat, m, lat_row, g, w_in, seg)
            if need_ctx:
                raise NotImplementedError("context outputs of the neighbourhood mixer are not needed at this depth")
            k_c, v_c = _project(cx, m, ctx_row, g, w_in[:, c_width:], seg[1:])
            o_l = _neighbourhood_attention_rows(q_l, k_l, v_l, k_c, v_c, na_rpb[i])
            pre_l = ([o_l], w_out)
        lat = _ffn(lat, m, lat_row, g, w1, w3, w2, (layer, 1), 2, pre=pre_l,
                   final_w=final_norm_w if last else None)
        if need_ctx:
            cx = _ffn(cx, m, ctx_row, g, w1, w3, w2, (layer, 1), 2, pre=pre_c)
    return lat
```

```python
import functools
import math

import jax
import jax.numpy as jnp
from jax import lax
from jax.experimental import pallas as pl
from jax.experimental.pallas import tpu as pltpu

F32 = jnp.float32
BF16 = jnp.bfloat16
HIGHEST = lax.Precision.HIGHEST

GRID_W = 64
N_MOD = 9
NORM_EPS = 1e-6
A_HEADS = 4
A_QK_DIM = 64
A_V_DIM = 128
HY_EMB = 33
HY_BANDS = 16
HY_TARGET = 1e-2
HY_MAX_DECAY = math.log(HY_TARGET) / 0.3
HY_MIN_DECAY = math.log(HY_TARGET) / 1.5
C_HEADS = 16
C_HEAD_DIM = 64
NA_ROWS = 8
NA_COLS = 16
ROPE_BASE = 10000.0

LANES = 128
SUBLANES = 8
VMEM_LIMIT = 56 * 2**20

NA_QROWS = 4
NA_KROWS = NA_QROWS + NA_ROWS
NEG = -1e30
ATTN_CHUNK = 512
SCORE_ROWS = 64
LOG2E = math.log2(math.e)

DFT_L = 128
P1 = DFT_L + 8
LOOP_UNROLL = 8
SMALL_LOOP_UNROLL = 16


def _cparams(sem):
    return pltpu.CompilerParams(dimension_semantics=sem, vmem_limit_bytes=VMEM_LIMIT)


def _const_spec(shape):
    nd = len(shape)
    return pl.BlockSpec(shape, lambda *_: (0,) * nd, pipeline_mode=pl.Buffered(1))


def _rms(x, w):
    ms = jnp.mean(x * x, axis=-1, keepdims=True)
    return x * lax.rsqrt(ms + NORM_EPS) * w


def _sigmoid(x):
    return 1.0 / (1.0 + jnp.exp(-x))


def _mod_kernel(s_ref, w_ref, b_ref, o_ref):
    s = s_ref[...]
    s = s * _sigmoid(s)
    o_ref[0] = jnp.dot(s, w_ref[0], precision=HIGHEST, preferred_element_type=F32) + b_ref[0]


def _modulation(c, c_ctx, mod_w, mod_b):
    depth, d, nd = mod_w.shape
    bsz = c.shape[0]
    rows = -(-(bsz + 1) // SUBLANES) * SUBLANES
    s = jnp.zeros((rows, d), F32).at[:bsz].set(c).at[bsz].set(c_ctx)
    tn = 1024
    out = pl.pallas_call(
        _mod_kernel,
        name="modulation",
        grid=(depth, nd // tn),
        in_specs=[
            pl.BlockSpec((rows, d), lambda l, j: (0, 0)),
            pl.BlockSpec((1, d, tn), lambda l, j: (l, 0, j)),
            pl.BlockSpec((1, 1, tn), lambda l, j: (l, 0, j)),
        ],
        out_specs=pl.BlockSpec((1, rows, tn), lambda l, j: (l, 0, j)),
        out_shape=jax.ShapeDtypeStruct((depth, rows, nd), F32),
        compiler_params=_cparams(("arbitrary", "arbitrary")),
    )(s, mod_w, mod_b.reshape(depth, 1, nd))
    return out[:, :bsz + 1].reshape(depth, bsz + 1, N_MOD, d)


def _ffn_kernel(*refs, k, pre_widths, final):
    it = iter(refs)
    x_ref, mod_ref, g_ref, w1_ref, w3_ref, w2_ref = (next(it) for _ in range(6))
    y_refs = [next(it) for _ in pre_widths]
    wo_ref = next(it) if pre_widths else None
    fw_ref = next(it) if final else None
    o_ref = next(it)

    x = x_ref[0]
    if pre_widths:
        off = 0
        y = None
        for y_ref, w in zip(y_refs, pre_widths):
            t = jnp.dot(y_ref[0], wo_ref[off:off + w, :], preferred_element_type=F32)
            y = t if y is None else y + t
            off += w
        x = x + mod_ref[0, 5:6, :] * y
    shift = mod_ref[0, 3 * k:3 * k + 1, :]
    scale = mod_ref[0, 3 * k + 1:3 * k + 2, :]
    gate = mod_ref[0, 3 * k + 2:3 * k + 3, :]
    hn = (_rms(x, g_ref[k:k + 1, :]) * (1.0 + scale) + shift).astype(BF16)
    a = jnp.dot(hn, w1_ref[...], preferred_element_type=F32)
    b = jnp.dot(hn, w3_ref[...], preferred_element_type=F32)
    act = (a * _sigmoid(a) * b).astype(BF16)
    out = x + 0.5 * gate * jnp.dot(act, w2_ref[...], preferred_element_type=F32)
    if final:
        out = _rms(out, fw_ref[...])
    o_ref[0] = out


def _ffn(x, mod, mod_row, g, w1, w3, w2, widx, k, pre=None, final_w=None):
    bsz, n, d = x.shape
    d_ff = w1.shape[-1]
    tm = min(n, 512)
    row = lambda b, i: (b, i, 0)
    pick = lambda b, i: (widx[0], widx[1], 0, 0)
    in_specs = [
        pl.BlockSpec((1, tm, d), row),
        pl.BlockSpec((1, N_MOD, d), lambda b, i: (mod_row(b), 0, 0)),
        _const_spec(g.shape),
        pl.BlockSpec((None, None, d, d_ff), pick, pipeline_mode=pl.Buffered(1)),
        pl.BlockSpec((None, None, d, d_ff), pick, pipeline_mode=pl.Buffered(1)),
        pl.BlockSpec((None, None, d_ff, d), pick, pipeline_mode=pl.Buffered(1)),
    ]
    args = [x, mod, g, w1, w3, w2]
    pre_widths = ()
    if pre is not None:
        ys, wo = pre
        pre_widths = tuple(y.shape[-1] for y in ys)
        for y in ys:
            in_specs.append(pl.BlockSpec((1, tm, y.shape[-1]), row))
            args.append(y)
        in_specs.append(_const_spec(wo.shape))
        args.append(wo)
    if final_w is not None:
        in_specs.append(_const_spec((1, d)))
        args.append(final_w.reshape(1, d))
    return pl.pallas_call(
        functools.partial(_ffn_kernel, k=k, pre_widths=pre_widths, final=final_w is not None),
        name="macaron_ffn",
        grid=(bsz, n // tm),
        in_specs=in_specs,
        out_specs=pl.BlockSpec((1, tm, d), row),
        out_shape=jax.ShapeDtypeStruct((bsz, n, d), F32),
        compiler_params=_cparams(("parallel", "parallel")),
    )(*args)


def _rope(u, cos, sin_signed, axis):
    w = u.shape[axis]
    idx = lax.broadcasted_iota(jnp.int32, u.shape, axis)
    partner = jnp.where(idx % 32 < 16, pltpu.roll(u, w - 16, axis), pltpu.roll(u, 16, axis))
    return u * cos + partner * sin_signed


def _proj_kernel(*refs, segments, use_rope, any_major):
    it = iter(refs)
    x_ref, mod_ref, g_ref, w_ref = (next(it) for _ in range(4))
    wt_ref = next(it) if any_major else None
    cos_ref, sin_ref, cost_ref, sint_ref = (next(it) if use_rope else None for _ in range(4))
    out_refs = list(it)
    x = x_ref[0]
    hn = (_rms(x, g_ref[1:2, :]) * (1.0 + mod_ref[0, 4:5, :]) + mod_ref[0, 3:4, :]).astype(BF16)
    off = 0
    toff = 0
    for o_ref, (width, scale, rope, tchunk) in zip(out_refs, segments):
        if tchunk:
            ut = lax.dot_general(wt_ref[toff:toff + width, :], hn, (((1,), (1,)), ((), ())),
                                 preferred_element_type=F32)
            if rope:
                ut = _rope(ut, cost_ref[...], sint_ref[...], 0)
            if scale != 1.0:
                ut = ut * scale
            ut = ut.astype(o_ref.dtype)
            for j in range(ut.shape[1] // tchunk):
                o_ref[0, j] = ut[:, j * tchunk:(j + 1) * tchunk]
            toff += width
        else:
            u = jnp.dot(hn, w_ref[:, off:off + width], preferred_element_type=F32)
            if rope:
                u = _rope(u, cos_ref[...], sin_ref[...], 1)
            if scale != 1.0:
                u = u * scale
            o_ref[0] = u.astype(o_ref.dtype)
        off += width


def _project(x, mod, mod_row, g, w, segments, rope_tables=None):
    bsz, n, d = x.shape
    tm = min(n, 512)
    row = lambda b, i: (b, i, 0)
    out_specs, out_shape, wt_parts = [], [], []
    off = 0
    for width, _, _, tchunk in segments:
        if tchunk:
            tchunk = min(tchunk, tm)
            out_specs.append(pl.BlockSpec((1, tm // tchunk, width, tchunk), lambda b, i: (b, i, 0, 0)))
            out_shape.append(jax.ShapeDtypeStruct((bsz, n // tchunk, width, tchunk), BF16))
            wt_parts.append(w[:, off:off + width].T)
        else:
            out_specs.append(pl.BlockSpec((1, tm, width), row))
            out_shape.append(jax.ShapeDtypeStruct((bsz, n, width), BF16))
        off += width
    segments = tuple((w_, s_, r_, min(t_, tm)) for w_, s_, r_, t_ in segments)
    in_specs = [
        pl.BlockSpec((1, tm, d), row),
        pl.BlockSpec((1, N_MOD, d), lambda b, i: (mod_row(b), 0, 0)),
        _const_spec(g.shape),
        _const_spec(w.shape),
    ]
    args = [x, mod, g, w]
    if wt_parts:
        wt = jnp.concatenate(wt_parts, axis=0)
        in_specs.append(_const_spec(wt.shape))
        args.append(wt)
    if rope_tables is not None:
        for t in rope_tables:
            in_specs.append(pl.BlockSpec((tm, t.shape[1]), lambda b, i: (i, 0)))
            args.append(t)
        for t in rope_tables:
            in_specs.append(pl.BlockSpec((t.shape[1], tm), lambda b, i: (0, i)))
            args.append(t.T)
    return pl.pallas_call(
        functools.partial(_proj_kernel, segments=segments, use_rope=rope_tables is not None,
                          any_major=bool(wt_parts)),
        name="mixer_projection",
        grid=(bsz, n // tm),
        in_specs=in_specs,
        out_specs=out_specs,
        out_shape=out_shape,
        compiler_params=_cparams(("parallel", "parallel")),
    )(*args)


def _rope_tables(n, reps):
    t = jnp.arange(n)
    nf = A_QK_DIM // 4
    inv_freq = ROPE_BASE ** (-jnp.arange(nf, dtype=F32) / nf)
    cols, sins = [], []
    for pos in (t // GRID_W, t % GRID_W):
        ang = pos.astype(F32)[:, None] * inv_freq
        c, s = jnp.cos(ang), jnp.sin(ang)
        cols += [c, c]
        sins += [-s, s]
    return jnp.tile(jnp.concatenate(cols, 1), (1, reps)), jnp.tile(jnp.concatenate(sins, 1), (1, reps))


def _exp_rows(s_ref, e_ref, rows, m):
    for r in range(0, rows, SCORE_ROWS):
        e_ref[r:r + SCORE_ROWS, :] = jnp.exp2(s_ref[r:r + SCORE_ROWS, :] - m).astype(BF16)


def _dattn_rows_kernel(*refs, n_src, lam_init):
    lam_ref, sw_ref, q_ref = refs[:3]
    k_refs = refs[3:3 + n_src]
    v_refs = refs[3 + n_src:3 + 2 * n_src]
    o_ref = refs[3 + 2 * n_src]
    q = q_ref[0]
    lane = lax.broadcasted_iota(jnp.int32, q.shape, 1)
    zero = jnp.zeros_like(q)
    nt = (((1,), (1,)), ((), ()))
    qms = (jnp.where(lane < A_QK_DIM, q, zero), jnp.where(lane >= A_QK_DIM, q, zero))
    scores = [[lax.dot_general(qm, k_ref[0], nt, preferred_element_type=F32) for k_ref in k_refs] for qm in qms]
    outs = []
    for ss in scores:
        m = None
        for s in ss:
            ms = jnp.max(s, axis=-1, keepdims=True)
            m = ms if m is None else jnp.maximum(m, ms)
        acc = None
        den = None
        for s, v_ref in zip(ss, v_refs):
            e = jnp.exp2(s - m)
            ds = jnp.sum(e, axis=-1, keepdims=True)
            t = jnp.dot(e.astype(BF16), v_ref[0], preferred_element_type=F32)
            den = ds if den is None else den + ds
            acc = t if acc is None else acc + t
        outs.append(acc / den)
    lp = lam_ref[...]
    lam = (jnp.exp(jnp.sum(lp[0:1] * lp[1:2], axis=-1, keepdims=True))
           - jnp.exp(jnp.sum(lp[2:3] * lp[3:4], axis=-1, keepdims=True)) + lam_init)
    o = outs[0] - lam * outs[1]
    o_ref[0] = (_rms(o, sw_ref[...]) * (1.0 - lam_init)).astype(o_ref.dtype)


def _diff_attention_rows(q, ks, vs, lam_p, subln_w, lam_init):
    bsz, nq, width = q.shape
    tq = min(nq, 512)
    hb = lambda b, h, i: (b, i, h)
    kv = lambda b, h, i: (b, 0, h)
    in_specs = [
        pl.BlockSpec(lam_p.shape, lambda b, h, i: (0, 0)),
        pl.BlockSpec((1, A_V_DIM), lambda b, h, i: (0, 0)),
        pl.BlockSpec((1, tq, LANES), hb),
    ]
    in_specs += [pl.BlockSpec((1, k.shape[1], LANES), kv) for k in ks]
    in_specs += [pl.BlockSpec((1, v.shape[1], LANES), kv) for v in vs]
    return pl.pallas_call(
        functools.partial(_dattn_rows_kernel, n_src=len(ks), lam_init=lam_init),
        name="diff_attention_rows",
        grid=(bsz, A_HEADS, nq // tq),
        in_specs=in_specs,
        out_specs=pl.BlockSpec((1, tq, LANES), hb),
        out_shape=jax.ShapeDtypeStruct((bsz, nq, width), BF16),
        compiler_params=_cparams(("parallel", "parallel", "parallel")),
    )(lam_p, subln_w.reshape(1, A_V_DIM), q, *ks, *vs)


def _na_kernel(qt_ref, k_ref, vt_ref, kc_ref, vct_ref, tb_ref, o_ref, s_ref, e_ref, *, rows):
    i = pl.program_id(2)
    r0 = i * NA_QROWS
    start = jnp.clip(r0 - NA_ROWS // 2, 0, rows - NA_KROWS)
    koff = pl.multiple_of(start * GRID_W, 2 * GRID_W)
    kw = k_ref[0, pl.ds(koff, NA_KROWS * GRID_W), :]
    c0 = start // 2
    vwt = jnp.concatenate([vt_ref[0, c0 + j] for j in range(NA_KROWS // 2)], axis=1)
    kc = kc_ref[0]
    vct = jnp.concatenate([vct_ref[0, j] for j in range(vct_ref.shape[1])], axis=1)
    qt = qt_ref[0, 0]
    row = lax.broadcasted_iota(jnp.int32, qt.shape, 0)
    zero = jnp.zeros_like(qt)
    lane_b = lax.broadcasted_iota(jnp.int32, (GRID_W, LANES), 1)
    n_lat, n_all = kw.shape[0], kw.shape[0] + kc.shape[0]
    vt_all = jnp.concatenate([vwt, vct], axis=1)
    ones = jnp.ones((2 * SUBLANES, n_all), BF16)
    qms = (jnp.where(row < C_HEAD_DIM, qt, zero), jnp.where(row >= C_HEAD_DIM, qt, zero))
    for hh, qm in enumerate(qms):
        s_ref[hh, 0:n_lat, :] = jnp.dot(kw, qm, preferred_element_type=F32)
        s_ref[hh, n_lat:n_all, :] = jnp.dot(kc, qm, preferred_element_type=F32)
    outs = []
    for hh in range(2):
        mx = jnp.full((1, qt.shape[1]), NEG, F32)
        for m in range(NA_KROWS):
            kr = start + m
            blocks = []
            for jp in range(NA_QROWS // 2):
                idx = []
                for j in (2 * jp, 2 * jp + 1):
                    r = r0 + j
                    rs = jnp.clip(r - NA_ROWS // 2, 0, rows - NA_ROWS)
                    valid = jnp.logical_and(kr >= rs, kr < rs + NA_ROWS)
                    idx.append(jnp.where(valid, kr - r + NA_ROWS - 1, 2 * NA_ROWS - 1))
                blocks.append(jnp.where(lane_b < GRID_W, tb_ref[hh, idx[0]], tb_ref[hh, idx[1]]))
            blk = s_ref[hh, m * GRID_W:(m + 1) * GRID_W, :] + jnp.concatenate(blocks, axis=1)
            s_ref[hh, m * GRID_W:(m + 1) * GRID_W, :] = blk
            mx = jnp.maximum(mx, jnp.max(blk, axis=0, keepdims=True))
        for r in range(n_lat, n_all, SCORE_ROWS):
            mx = jnp.maximum(mx, jnp.max(s_ref[hh, r:r + SCORE_ROWS, :], axis=0, keepdims=True))
        _exp_rows(s_ref.at[hh], e_ref.at[hh], n_all, mx)
        e = e_ref[hh]
        acc = jnp.dot(vt_all, e, preferred_element_type=F32)
        l = jnp.dot(ones, e, preferred_element_type=F32)
        outs.append(acc / l[0:1])
    o_ref[0] = jnp.where(row < C_HEAD_DIM, outs[0], outs[1]).T.astype(o_ref.dtype)


def _na_rows_kernel(q_ref, k_ref, v_ref, kc_ref, vc_ref, tb_ref, o_ref, *, rows):
    i = pl.program_id(2)
    r0 = i * NA_QROWS
    start = jnp.clip(r0 - NA_ROWS // 2, 0, rows - NA_KROWS)
    koff = pl.multiple_of(start * GRID_W, GRID_W)
    kw = k_ref[0, pl.ds(koff, NA_KROWS * GRID_W), :]
    vw = v_ref[0, pl.ds(koff, NA_KROWS * GRID_W), :]
    kc = kc_ref[0]
    vc = vc_ref[0]
    q = q_ref[0]
    lane = lax.broadcasted_iota(jnp.int32, q.shape, 1)
    zero = jnp.zeros_like(q)
    lane_b = lax.broadcasted_iota(jnp.int32, (GRID_W, LANES), 1)
    nt = (((1,), (1,)), ((), ()))
    qms = (jnp.where(lane < C_HEAD_DIM, q, zero), jnp.where(lane >= C_HEAD_DIM, q, zero))
    scores = [(lax.dot_general(qm, kw, nt, preferred_element_type=F32),
               lax.dot_general(qm, kc, nt, preferred_element_type=F32)) for qm in qms]
    outs = []
    for hh, (s_lat, s_ctx) in enumerate(scores):
        bias_rows = []
        for j in range(NA_QROWS):
            r = r0 + j
            rs = jnp.clip(r - NA_ROWS // 2, 0, rows - NA_ROWS)
            blocks = []
            for mp in range(NA_KROWS // 2):
                idx = []
                for m in (2 * mp, 2 * mp + 1):
                    kr = start + m
                    valid = jnp.logical_and(kr >= rs, kr < rs + NA_ROWS)
                    idx.append(jnp.where(valid, kr - r + NA_ROWS - 1, 2 * NA_ROWS - 1))
                blocks.append(jnp.where(lane_b < GRID_W, tb_ref[hh, idx[0]], tb_ref[hh, idx[1]]))
            bias_rows.append(jnp.concatenate(blocks, axis=1))
        s_lat = s_lat + jnp.concatenate(bias_rows, axis=0)
        m = jnp.maximum(jnp.max(s_lat, axis=-1, keepdims=True), jnp.max(s_ctx, axis=-1, keepdims=True))
        e_lat = jnp.exp2(s_lat - m)
        e_ctx = jnp.exp2(s_ctx - m)
        denom = jnp.sum(e_lat, axis=-1, keepdims=True) + jnp.sum(e_ctx, axis=-1, keepdims=True)
        o = (jnp.dot(e_lat.astype(BF16), vw, preferred_element_type=F32)
             + jnp.dot(e_ctx.astype(BF16), vc, preferred_element_type=F32))
        outs.append(o / denom)
    o_ref[0] = jnp.where(lane < C_HEAD_DIM, outs[0], outs[1]).astype(o_ref.dtype)


def _neighbourhood_attention_rows(q, k, v, k_ctx, v_ctx, rpb):
    bsz, n, width = q.shape
    rows = n // GRID_W
    assert rows >= NA_KROWS and rows % NA_QROWS == 0
    n_ctx = k_ctx.shape[1]
    tq = NA_QROWS * GRID_W
    tb = _na_bias_table(rpb, key_major=False)
    kv = lambda b, h, i: (b, 0, h)
    return pl.pallas_call(
        functools.partial(_na_rows_kernel, rows=rows),
        name="neighbourhood_attention_rows",
        grid=(bsz, C_HEADS // 2, n // tq),
        in_specs=[
            pl.BlockSpec((1, tq, LANES), lambda b, h, i: (b, i, h)),
            pl.BlockSpec((1, n, LANES), kv),
            pl.BlockSpec((1, n, LANES), kv),
            pl.BlockSpec((1, n_ctx, LANES), kv),
            pl.BlockSpec((1, n_ctx, LANES), kv),
            pl.BlockSpec((2, 2 * NA_ROWS, GRID_W, LANES), lambda b, h, i: (h, 0, 0, 0)),
        ],
        out_specs=pl.BlockSpec((1, tq, LANES), lambda b, h, i: (b, i, h)),
        out_shape=jax.ShapeDtypeStruct((bsz, n, width), BF16),
        compiler_params=_cparams(("parallel", "parallel", "arbitrary")),
    )(q, k, v, k_ctx, v_ctx, tb)


def _na_bias_table(rpb, key_major=True):
    nh = rpb.shape[0]
    c = jnp.arange(GRID_W)
    cs = jnp.clip(c - NA_COLS // 2, 0, GRID_W - NA_COLS)
    kc = jnp.arange(GRID_W)
    inside = (kc[None, :] >= cs[:, None]) & (kc[None, :] < cs[:, None] + NA_COLS)
    off = jnp.clip(kc[None, :] - c[:, None] + NA_COLS - 1, 0, 2 * NA_COLS - 2)
    tb = jnp.where(inside[None, None], rpb[:, :, off] * LOG2E, NEG)
    tb = jnp.concatenate([tb, jnp.full((nh, 1, GRID_W, GRID_W), NEG, F32)], axis=1)
    if key_major:
        tb = jnp.swapaxes(tb, -1, -2)
    return jnp.concatenate([tb, tb], axis=-1)


def _neighbourhood_attention(qt, k, vt, k_ctx, vt_ctx, rpb):
    bsz, n, width = k.shape
    rows = n // GRID_W
    assert rows >= NA_KROWS and rows % NA_QROWS == 0 and NA_QROWS % 2 == 0 and NA_ROWS % 4 == 0
    n_ctx = k_ctx.shape[1]
    tq = NA_QROWS * GRID_W
    n_keys = NA_KROWS * GRID_W + n_ctx
    per_chunk = qt.shape[3] // tq
    assert vt.shape[3] == 2 * GRID_W and per_chunk * tq == qt.shape[3]
    tb = _na_bias_table(rpb)
    kv = lambda b, h, i: (b, 0, h)
    kvt = lambda b, h, i: (b, 0, h, 0)
    return pl.pallas_call(
        functools.partial(_na_kernel, rows=rows),
        name="neighbourhood_attention",
        grid=(bsz, C_HEADS // 2, n // tq),
        in_specs=[
            pl.BlockSpec((1, 1, LANES, tq), lambda b, h, i: (b, i // per_chunk, h, i % per_chunk)),
            pl.BlockSpec((1, n, LANES), kv),
            pl.BlockSpec((1, vt.shape[1], LANES, vt.shape[3]), kvt),
            pl.BlockSpec((1, n_ctx, LANES), kv),
            pl.BlockSpec((1, vt_ctx.shape[1], LANES, vt_ctx.shape[3]), kvt),
            pl.BlockSpec((2, 2 * NA_ROWS, GRID_W, LANES), lambda b, h, i: (h, 0, 0, 0)),
        ],
        out_specs=pl.BlockSpec((1, tq, LANES), lambda b, h, i: (b, i, h)),
        out_shape=jax.ShapeDtypeStruct((bsz, n, width), BF16),
        scratch_shapes=[pltpu.VMEM((2, n_keys, tq), F32), pltpu.VMEM((2, n_keys, tq), BF16)],
        compiler_params=_cparams(("parallel", "parallel", "arbitrary")),
    )(qt, k, vt, k_ctx, vt_ctx, tb)


def _filter_kernel(z_ref, zr_ref, w0_ref, b0_ref, w1_ref, b1_ref, fr_ref, wf_ref, wb_ref, dl_ref,
                   first_ref, second_ref, *, n_inner, cw):
    fr = fr_ref[...]
    tt = z_ref.shape[0]
    row = lax.broadcasted_iota(jnp.int32, (tt, cw), 0) + pl.program_id(0) * tt
    for pos_ref, wo_ref, o_ref, drop_row0 in ((z_ref, wf_ref, first_ref, False), (zr_ref, wb_ref, second_ref, True)):
        z = pos_ref[...]
        hid = jnp.sin(fr * (jnp.dot(z, w0_ref[...], precision=HIGHEST, preferred_element_type=F32) + b0_ref[...]))
        for i in range(n_inner):
            hid = jnp.sin(fr * (jnp.dot(hid, w1_ref[i], precision=HIGHEST, preferred_element_type=F32)
                                + b1_ref[i:i + 1, :]))
        filt = jnp.dot(hid, wo_ref[...], precision=HIGHEST, preferred_element_type=F32)
        window = jnp.exp(-z[:, 0:1] * dl_ref[...])
        if drop_row0:
            window = jnp.where(row == 0, 0.0, window)
        for j in range(filt.shape[1] // cw):
            o_ref[:, j * cw:(j + 1) * cw] = filt[:, j * cw:(j + 1) * cw] * window


def _filter_features(pos, n):
    t = (pos.astype(F32) / (n - 1))[:, None]
    w = (2.0 * math.pi / n) * pos.astype(F32)[:, None]
    bands = jnp.linspace(1e-4, HY_BANDS - 1, HY_BANDS, dtype=F32)[None, :]
    z = jnp.concatenate([t, jnp.cos(bands * w), -jnp.sin(bands * w)], axis=-1)
    return jnp.pad(z, ((0, 0), (0, LANES - HY_EMB)))


def _hyena_filters(n, f_w0, f_b0, f_w1, f_b1, f_freq, f_wout, cw):
    pos = jnp.arange(n, dtype=jnp.int32)
    z = _filter_features(pos, n)
    zr = _filter_features((n - pos) % n, n)
    w0 = jnp.pad(f_w0, ((0, LANES - HY_EMB), (0, 0)))
    deltas = jnp.abs(jnp.linspace(HY_MIN_DECAY, HY_MAX_DECAY, cw, dtype=F32))[None, :]
    hid = f_w0.shape[1]
    orders = f_wout.shape[1] // (2 * cw)
    wo = f_wout.reshape(hid, orders, 2, cw)
    w_fwd = wo[:, :, 0].reshape(hid, orders * cw)
    w_bwd = wo[:, :, 1].reshape(hid, orders * cw)
    tt = min(n, 512)
    full = lambda shape: pl.BlockSpec(shape, lambda i: (0,) * len(shape))
    out_spec = pl.BlockSpec((tt, orders * cw), lambda i: (i, 0))
    out_shape = jax.ShapeDtypeStruct((n, orders * cw), F32)
    return pl.pallas_call(
        functools.partial(_filter_kernel, n_inner=f_w1.shape[0], cw=cw),
        name="hyena_filters",
        grid=(n // tt,),
        in_specs=[
            pl.BlockSpec((tt, LANES), lambda i: (i, 0)),
            pl.BlockSpec((tt, LANES), lambda i: (i, 0)),
            full(w0.shape), full((1, hid)), full(f_w1.shape), full(f_b1.shape), full((1, hid)),
            full(w_fwd.shape), full(w_bwd.shape), full((1, cw)),
        ],
        out_specs=[out_spec, out_spec],
        out_shape=[out_shape, out_shape],
        compiler_params=_cparams(("parallel",)),
    )(z, zr, w0, f_b0.reshape(1, hid), f_w1, f_b1, f_freq.reshape(1, hid), w_fwd, w_bwd, deltas)


def _cis(phase, period, sign):
    ang = (2.0 * math.pi / period) * (phase % period).astype(F32)
    return jnp.cos(ang), sign * jnp.sin(ang)


def _cblock(mr, mi):
    return jnp.concatenate([jnp.concatenate([mr, -mi], -1), jnp.concatenate([mi, mr], -1)], -2)


def _dft_tables(n):
    big = 2 * n
    nb = big // DFT_L
    f = jnp.arange(DFT_L, dtype=jnp.int32)
    k1 = jnp.arange(nb, dtype=jnp.int32)
    blk = jnp.arange(nb, dtype=jnp.int32)
    ph1 = f[:, None, None] * k1[None, :, None] + DFT_L * blk[None, None, :] * k1[None, :, None]
    cr, ci = _cis(ph1, big, -1.0)
    t1_data = _cblock(cr[:, :, :nb // 2], ci[:, :, :nb // 2])
    t1_real = jnp.concatenate([cr, ci], axis=1)
    ph2 = f[:, None] * f[None, :]
    fr, fi = _cis(ph2, DFT_L, -1.0)
    t2 = _cblock(fr, fi)
    t2_inv = _cblock(fr, -fi)
    n1 = jnp.arange(nb // 2, dtype=jnp.int32)
    ph3 = DFT_L * n1[None, :, None] * k1[None, None, :] + f[:, None, None] * k1[None, None, :]
    dr, di = _cis(ph3, big, 1.0)
    t3 = _cblock(dr, di) / big
    return t1_data.astype(BF16), t1_real.astype(BF16), t2.astype(BF16), t2_inv.astype(BF16), t3.astype(BF16)


def _dft_stage1(xa_ref, xb_ref, t1_ref, a_ref, nb):
    half = nb // 2

    def body(f, carry):
        rhs = jnp.concatenate([xa_ref[pl.ds(f, half, stride=DFT_L), :],
                               xb_ref[pl.ds(f, half, stride=DFT_L), :]], axis=0).astype(BF16)
        out = jnp.dot(t1_ref[f], rhs, preferred_element_type=F32)
        a_ref[pl.ds(pl.multiple_of(f * P1, SUBLANES), 2 * nb), :] = out
        return carry

    lax.fori_loop(0, DFT_L, body, 0, unroll=SMALL_LOOP_UNROLL)


def _dft_stage2_in(a_ref, k1, nb):
    return jnp.concatenate([a_ref[pl.ds(k1, DFT_L, stride=P1), :],
                            a_ref[pl.ds(nb + k1, DFT_L, stride=P1), :]], axis=0).astype(BF16)


def _spectrum_kernel(xa_ref, xb_ref, t1_ref, t2_ref, o_ref, a_ref, *, nb):
    _dft_stage1(xa_ref, xb_ref, t1_ref, a_ref, nb)

    def body(k1, carry):
        out = jnp.dot(t2_ref[...], _dft_stage2_in(a_ref, k1, nb), preferred_element_type=F32)
        o_ref[pl.ds(pl.multiple_of(k1 * 2 * DFT_L, 2 * DFT_L), 2 * DFT_L), :] = out
        return carry

    lax.fori_loop(0, nb, body, 0, unroll=LOOP_UNROLL)


def _filter_spectrum(first, second, t1_real, t2):
    n, cols = first.shape
    nb = 2 * n // DFT_L
    col = lambda j: (0, j)
    return pl.pallas_call(
        functools.partial(_spectrum_kernel, nb=nb),
        name="filter_spectrum",
        grid=(cols // LANES,),
        in_specs=[
            pl.BlockSpec((n, LANES), col),
            pl.BlockSpec((n, LANES), col),
            _const_spec(t1_real.shape),
            _const_spec(t2.shape),
        ],
        out_specs=pl.BlockSpec((4 * n, LANES), col),
        out_shape=jax.ShapeDtypeStruct((4 * n, cols), F32),
        scratch_shapes=[pltpu.VMEM((DFT_L * P1, LANES), F32)],
        compiler_params=_cparams(("parallel",)),
    )(first, second, t1_real, t2)


def _short_conv_kernel(u_ref, w_ref, b_ref, o_ref):
    u = u_ref[0].astype(F32)
    n = u.shape[0]
    w0, w1, w2 = w_ref[0:1, :], w_ref[1:2, :], w_ref[2:3, :]
    out = pltpu.roll(u, 1, 0) * w0 + u * w1 + pltpu.roll(u, n - 1, 0) * w2 + b_ref[...]
    row = lax.broadcasted_iota(jnp.int32, (SUBLANES, u.shape[1]), 0)
    head = out[0:SUBLANES] - jnp.where(row == 0, u[n - 1:n] * w0, 0.0)
    tail = out[n - SUBLANES:n] - jnp.where(row == SUBLANES - 1, u[0:1] * w2, 0.0)
    o_ref[0] = jnp.concatenate([head, out[SUBLANES:n - SUBLANES], tail], axis=0).astype(o_ref.dtype)


def _short_conv(u, u_blk, w, b):
    bsz, n, _ = u.shape
    taps, ch = w.shape
    cb = 2 * LANES
    assert ch % cb == 0 and (u_blk * LANES) % cb == 0
    first = u_blk * LANES // cb
    return pl.pallas_call(
        _short_conv_kernel,
        name="short_conv",
        grid=(bsz, ch // cb),
        in_specs=[
            pl.BlockSpec((1, n, cb), lambda bi, j: (bi, 0, first + j)),
            pl.BlockSpec((taps, cb), lambda bi, j: (0, j)),
            pl.BlockSpec((1, cb), lambda bi, j: (0, j)),
        ],
        out_specs=pl.BlockSpec((1, n, cb), lambda bi, j: (bi, 0, j)),
        out_shape=jax.ShapeDtypeStruct((bsz, n, ch), BF16),
        compiler_params=_cparams(("parallel", "parallel")),
    )(u, w, b.reshape(1, ch))


def _cmul_store(s, g, rows):
    sr, si = s[:rows], s[rows:]
    gr, gi = g[:rows], g[rows:]
    return jnp.concatenate([sr * gr - si * gi, sr * gi + si * gr], 0).astype(BF16)


def _hyena_conv_kernel(gate_ref, v_ref, skip_ref, g_ref, t1_ref, t2_ref, t2i_ref, t3_ref, o_ref,
                       va_ref, vb_ref, a_ref, p_ref, y_ref, *, nb, p2, p3):
    half = nb // 2
    two_l = 2 * DFT_L
    va_ref[...] = v_ref[0].astype(F32)
    vb_ref[...] = v_ref[1].astype(F32)

    _dft_stage1(va_ref, vb_ref, t1_ref, a_ref, nb)

    def fwd2(k1, carry):
        s = jnp.dot(t2_ref[...], _dft_stage2_in(a_ref, k1, nb), preferred_element_type=F32)
        off = pl.multiple_of(k1 * two_l, two_l)
        p_ref[pl.ds(off, two_l), :] = _cmul_store(s, g_ref[pl.ds(off, two_l), :], DFT_L)
        return carry

    lax.fori_loop(0, nb, fwd2, 0, unroll=LOOP_UNROLL)

    def inv1(k1, carry):
        off = pl.multiple_of(k1 * two_l, two_l)
        out = jnp.dot(t2i_ref[...], p_ref[pl.ds(off, two_l), :], preferred_element_type=F32)
        a_ref[pl.ds(pl.multiple_of(k1 * p2, SUBLANES), two_l), :] = out
        return carry

    lax.fori_loop(0, nb, inv1, 0, unroll=LOOP_UNROLL)

    def inv2(n2, carry):
        rhs = jnp.concatenate([a_ref[pl.ds(n2, nb, stride=p2), :],
                               a_ref[pl.ds(DFT_L + n2, nb, stride=p2), :]], axis=0).astype(BF16)
        out = jnp.dot(t3_ref[n2], rhs, preferred_element_type=F32)
        y_ref[pl.ds(pl.multiple_of(n2 * p3, SUBLANES), nb), :] = out
        return carry

    lax.fori_loop(0, DFT_L, inv2, 0, unroll=SMALL_LOOP_UNROLL)

    skip = skip_ref[...]

    def epilogue(n1, carry):
        off = pl.multiple_of(n1 * DFT_L, DFT_L)
        for b, vs_ref in enumerate((va_ref, vb_ref)):
            y = y_ref[pl.ds(b * half + n1, DFT_L, stride=p3), :]
            gate = gate_ref[b, pl.ds(off, DFT_L), :].astype(F32)
            o_ref[b, pl.ds(off, DFT_L), :] = (gate * (y + skip * vs_ref[pl.ds(off, DFT_L), :])).astype(o_ref.dtype)
        return carry

    lax.fori_loop(0, half, epilogue, 0, unroll=LOOP_UNROLL)


def _pitch(rows):
    p = -(-rows // SUBLANES)
    if p % 2 == 0:
        p += 1
    return p * SUBLANES


def _conv_specs(n, gate_blk, v_blk, spec_rows, spec_blk):
    return [
        pl.BlockSpec((2, n, LANES), lambda j, p: (p, 0, gate_blk + j)),
        pl.BlockSpec((2, n, LANES), lambda j, p: (p, 0, v_blk + j)),
        pl.BlockSpec((1, LANES), lambda j, p: (0, j)),
        pl.BlockSpec((spec_rows, LANES), lambda j, p: (0, spec_blk + j), pipeline_mode=pl.Buffered(1)),
    ]


def _hyena_conv(gate, gate_blk, v, v_blk, skip, spec, spec_blk, tables):
    bsz, n, _ = v.shape
    cw = skip.shape[-1]
    out_shape = jax.ShapeDtypeStruct((bsz, n, cw), BF16)
    out_spec = pl.BlockSpec((2, n, LANES), lambda j, p: (p, 0, j))
    grid = (cw // LANES, bsz // 2)
    params = _cparams(("parallel", "arbitrary"))
    if len(tables) == 2:
        fwd, inv = tables
        return pl.pallas_call(
            _hyena_conv_small_kernel,
            name="long_conv_small",
            grid=grid,
            in_specs=_conv_specs(n, gate_blk, v_blk, 4 * n, spec_blk) + [_const_spec(fwd.shape), _const_spec(inv.shape)],
            out_specs=out_spec,
            out_shape=out_shape,
            compiler_params=params,
        )(gate, v, skip, spec, fwd, inv)
    nb = 2 * n // DFT_L
    t1, t2, t2i, t3 = tables
    p2, p3 = _pitch(2 * DFT_L), _pitch(nb)
    return pl.pallas_call(
        functools.partial(_hyena_conv_kernel, nb=nb, p2=p2, p3=p3),
        name="long_conv",
        grid=grid,
        in_specs=_conv_specs(n, gate_blk, v_blk, 4 * n, spec_blk) + [_const_spec(t.shape) for t in tables],
        out_specs=out_spec,
        out_shape=out_shape,
        scratch_shapes=[
            pltpu.VMEM((n, LANES), F32), pltpu.VMEM((n, LANES), F32),
            pltpu.VMEM((max(DFT_L * P1, nb * p2), LANES), F32),
            pltpu.VMEM((4 * n, LANES), BF16),
            pltpu.VMEM((DFT_L * p3, LANES), F32),
        ],
        compiler_params=params,
    )(gate, v, skip, spec, t1, t2, t2i, t3)


def _small_tables(n):
    big = 2 * n
    k = jnp.arange(big, dtype=jnp.int32)
    fr, fi = _cis(k[:, None] * k[None, :], big, -1.0)
    fwd = _cblock(fr[:, :n], fi[:, :n])
    real = jnp.concatenate([fr, fi], axis=0)
    inv = _cblock(fr[:n, :], -fi[:n, :]) / big
    return fwd.astype(BF16), real.astype(BF16), inv.astype(BF16)


def _spectrum_small_kernel(xa_ref, xb_ref, f_ref, o_ref):
    taps = jnp.concatenate([xa_ref[...], xb_ref[...]], axis=0).astype(BF16)
    o_ref[...] = jnp.dot(f_ref[...], taps, preferred_element_type=F32)


def _filter_spectrum_small(first, second, real):
    n, cols = first.shape
    col = lambda j: (0, j)
    return pl.pallas_call(
        _spectrum_small_kernel,
        name="filter_spectrum_small",
        grid=(cols // LANES,),
        in_specs=[pl.BlockSpec((n, LANES), col), pl.BlockSpec((n, LANES), col), _const_spec(real.shape)],
        out_specs=pl.BlockSpec((4 * n, LANES), col),
        out_shape=jax.ShapeDtypeStruct((4 * n, cols), F32),
        compiler_params=_cparams(("parallel",)),
    )(first, second, real)


def _hyena_conv_small_kernel(gate_ref, v_ref, skip_ref, g_ref, f_ref, i_ref, o_ref):
    n = v_ref.shape[1]
    rhs = jnp.concatenate([v_ref[0], v_ref[1]], axis=0)
    s = jnp.dot(f_ref[...], rhs, preferred_element_type=F32)
    y = jnp.dot(i_ref[...], _cmul_store(s, g_ref[...], 2 * n), preferred_element_type=F32)
    skip = skip_ref[...]
    for b in range(2):
        yb = y[b * n:(b + 1) * n] + skip * v_ref[b].astype(F32)
        o_ref[b] = (gate_ref[b].astype(F32) * yb).astype(o_ref.dtype)


def _hyena(u, u_blk, conv_w, conv_b, hy_b, spec, tables):
    cw = hy_b.shape[-1]
    cb = cw // LANES
    uc = _short_conv(u, u_blk, conv_w, conv_b)
    z = _hyena_conv(uc, 0, uc, 2 * cb, hy_b[0:1], spec, 0, tables)
    return _hyena_conv(uc, cb, z, 0, hy_b[1:2], spec, cb, tables)


def kernel(x, c, ctx, c_ctx, mod_w, mod_b, norm_w, ffn_w1, ffn_w3, ffn_w2, ab_w_in, ab_w_out, diff_lambda,
           diff_subln_w, hy_conv_w, hy_conv_b, hy_f_w0, hy_f_b0, hy_f_w1, hy_f_b1, hy_f_freq, hy_f_wout,
           hy_bias, na_w_in, na_w_out, na_rpb, final_norm_w):
    bsz, n, d = x.shape
    n_ctx = ctx.shape[1]
    depth = mod_w.shape[0]
    a_width = A_HEADS * A_V_DIM
    b_width = hy_bias.shape[-1]
    c_width = C_HEADS * C_HEAD_DIM
    qk_scale = A_QK_DIM ** -0.5 * LOG2E
    na_scale = C_HEAD_DIM ** -0.5 * LOG2E

    mod = _modulation(c, c_ctx, mod_w, mod_b)
    lat_row = lambda b: b
    ctx_row = lambda b: bsz
    w1, w3, w2 = ffn_w1.astype(BF16), ffn_w3.astype(BF16), ffn_w2.astype(BF16)

    lat, cx = x, ctx
    for layer in range(depth):
        need_ctx = layer < depth - 1
        last = layer == depth - 1
        m, g = mod[layer], norm_w[layer]
        i = layer // 2
        lat = _ffn(lat, m, lat_row, g, w1, w3, w2, (layer, 0), 0)
        cx = _ffn(cx, m, ctx_row, g, w1, w3, w2, (layer, 0), 0)
        if layer % 2 == 0:
            w_in = ab_w_in[i].astype(BF16)
            w_out = ab_w_out[i].astype(BF16)
            lam_init = 0.8 - 0.6 * math.exp(-0.3 * layer)
            seg = [(a_width, qk_scale, True, 0), (a_width, 1.0, True, 0), (a_width, 1.0, False, 0),
                   (3 * b_width, 1.0, False, 0)]
            q_l, k_l, v_l, u_l = _project(lat, m, lat_row, g, w_in, seg, _rope_tables(n, a_width // A_QK_DIM))
            seg_c = [(s[0], s[1], False, s[3]) for s in seg]
            if need_ctx:
                q_c, k_c, v_c, u_c = _project(cx, m, ctx_row, g, w_in, seg_c)
            else:
                k_c, v_c = _project(cx, m, ctx_row, g, w_in[:, a_width:3 * a_width], seg_c[1:3])
            a_l = _diff_attention_rows(q_l, [k_c, k_l], [v_c, v_l], diff_lambda[i], diff_subln_w[i], lam_init)

            def hy(u, length):
                first, second = _hyena_filters(length, hy_f_w0[i], hy_f_b0[i], hy_f_w1[i], hy_f_b1[i],
                                               hy_f_freq[i], hy_f_wout[i], b_width)
                if 2 * length // DFT_L < 2 * SUBLANES:
                    fwd, real, inv = _small_tables(length)
                    spec = _filter_spectrum_small(first, second, real)
                    tables = (fwd, inv)
                else:
                    t1d, t1r, t2, t2i, t3 = _dft_tables(length)
                    spec = _filter_spectrum(first, second, t1r, t2)
                    tables = (t1d, t2, t2i, t3)
                return _hyena(u, 0, hy_conv_w[i], hy_conv_b[i], hy_bias[i], spec, tables)

            b_l = hy(u_l, n)
            pre_l = ([a_l, b_l], w_out)
            if need_ctx:
                a_c = _diff_attention_rows(q_c, [k_c], [v_c], diff_lambda[i], diff_subln_w[i], lam_init)
                b_c = hy(u_c, n_ctx)
                pre_c = ([a_c, b_c], w_out)
        else:
            w_in = na_w_in[i].astype(BF16)
            w_out = na_w_out[i].astype(BF16)
            seg = [(c_width, na_scale, False, ATTN_CHUNK), (c_width, 1.0, False, 0), (c_width, 1.0, False, 2 * GRID_W)]
            q_l, k_l, v_l = _project(lat, m, lat_row, g, w_in, seg)
            if need_ctx:
                raise NotImplementedError("context outputs of the neighbourhood mixer are not needed at this depth")
            k_c, v_c = _project(cx, m, ctx_row, g, w_in[:, c_width:], seg[1:])
            o_l = _neighbourhood_attention(q_l, k_l, v_l, k_c, v_c, na_rpb[i])
            pre_l = ([o_l], w_out)
        lat = _ffn(lat, m, lat_row, g, w1, w3, w2, (layer, 1), 2, pre=pre_l,
                   final_w=final_norm_w if last else None)
        if need_ctx:
            cx = _ffn(cx, m, ctx_row, g, w1, w3, w2, (layer, 1), 2, pre=pre_c)
    return lat
```

```python
import functools
import math

import jax
import jax.numpy as jnp
from jax import lax
from jax.experimental import pallas as pl
from jax.experimental.pallas import tpu as pltpu

F32 = jnp.float32
BF16 = jnp.bfloat16
HIGHEST = lax.Precision.HIGHEST

GRID_W = 64
N_MOD = 9
NORM_EPS = 1e-6
A_HEADS = 4
A_QK_DIM = 64
A_V_DIM = 128
HY_EMB = 33
HY_BANDS = 16
HY_TARGET = 1e-2
HY_MAX_DECAY = math.log(HY_TARGET) / 0.3
HY_MIN_DECAY = math.log(HY_TARGET) / 1.5
C_HEADS = 16
C_HEAD_DIM = 64
NA_ROWS = 8
NA_COLS = 16
ROPE_BASE = 10000.0

LANES = 128
SUBLANES = 8
VMEM_LIMIT = 56 * 2**20

NA_QROWS = 4
NA_KROWS = NA_QROWS + NA_ROWS
NEG = -1e30
ATTN_CHUNK = 512
SCORE_ROWS = 64
LOG2E = math.log2(math.e)

DFT_L = 128
P1 = DFT_L + 8
LOOP_UNROLL = 8
SMALL_LOOP_UNROLL = 16


def _cparams(sem):
    return pltpu.CompilerParams(dimension_semantics=sem, vmem_limit_bytes=VMEM_LIMIT)


def _const_spec(shape):
    nd = len(shape)
    return pl.BlockSpec(shape, lambda *_: (0,) * nd, pipeline_mode=pl.Buffered(1))


def _rms(x, w):
    ms = jnp.mean(x * x, axis=-1, keepdims=True)
    return x * lax.rsqrt(ms + NORM_EPS) * w


def _sigmoid(x):
    return 1.0 / (1.0 + jnp.exp(-x))


def _mod_kernel(s_ref, w_ref, b_ref, o_ref):
    s = s_ref[...]
    s = s * _sigmoid(s)
    o_ref[0] = jnp.dot(s, w_ref[0], precision=HIGHEST, preferred_element_type=F32) + b_ref[0]


def _modulation(c, c_ctx, mod_w, mod_b):
    depth, d, nd = mod_w.shape
    bsz = c.shape[0]
    rows = -(-(bsz + 1) // SUBLANES) * SUBLANES
    s = jnp.zeros((rows, d), F32).at[:bsz].set(c).at[bsz].set(c_ctx)
    tn = 1024
    out = pl.pallas_call(
        _mod_kernel,
        name="modulation",
        grid=(depth, nd // tn),
        in_specs=[
            pl.BlockSpec((rows, d), lambda l, j: (0, 0)),
            pl.BlockSpec((1, d, tn), lambda l, j: (l, 0, j)),
            pl.BlockSpec((1, 1, tn), lambda l, j: (l, 0, j)),
        ],
        out_specs=pl.BlockSpec((1, rows, tn), lambda l, j: (l, 0, j)),
        out_shape=jax.ShapeDtypeStruct((depth, rows, nd), F32),
        compiler_params=_cparams(("arbitrary", "arbitrary")),
    )(s, mod_w, mod_b.reshape(depth, 1, nd))
    return out[:, :bsz + 1].reshape(depth, bsz + 1, N_MOD, d)


def _ffn_kernel(*refs, k, pre_widths, final):
    it = iter(refs)
    x_ref, mod_ref, g_ref, w1_ref, w3_ref, w2_ref = (next(it) for _ in range(6))
    y_refs = [next(it) for _ in pre_widths]
    wo_ref = next(it) if pre_widths else None
    fw_ref = next(it) if final else None
    o_ref = next(it)

    x = x_ref[0]
    if pre_widths:
        off = 0
        y = None
        for y_ref, w in zip(y_refs, pre_widths):
            t = jnp.dot(y_ref[0], wo_ref[off:off + w, :], preferred_element_type=F32)
            y = t if y is None else y + t
            off += w
        x = x + mod_ref[0, 5:6, :] * y
    shift = mod_ref[0, 3 * k:3 * k + 1, :]
    scale = mod_ref[0, 3 * k + 1:3 * k + 2, :]
    gate = mod_ref[0, 3 * k + 2:3 * k + 3, :]
    hn = (_rms(x, g_ref[k:k + 1, :]) * (1.0 + scale) + shift).astype(BF16)
    a = jnp.dot(hn, w1_ref[...], preferred_element_type=F32)
    b = jnp.dot(hn, w3_ref[...], preferred_element_type=F32)
    act = (a * _sigmoid(a) * b).astype(BF16)
    out = x + 0.5 * gate * jnp.dot(act, w2_ref[...], preferred_element_type=F32)
    if final:
        out = _rms(out, fw_ref[...])
    o_ref[0] = out


def _ffn(x, mod, mod_row, g, w1, w3, w2, widx, k, pre=None, final_w=None):
    bsz, n, d = x.shape
    d_ff = w1.shape[-1]
    tm = min(n, 512)
    row = lambda b, i: (b, i, 0)
    pick = lambda b, i: (widx[0], widx[1], 0, 0)
    in_specs = [
        pl.BlockSpec((1, tm, d), row),
        pl.BlockSpec((1, N_MOD, d), lambda b, i: (mod_row(b), 0, 0)),
        _const_spec(g.shape),
        pl.BlockSpec((None, None, d, d_ff), pick, pipeline_mode=pl.Buffered(1)),
        pl.BlockSpec((None, None, d, d_ff), pick, pipeline_mode=pl.Buffered(1)),
        pl.BlockSpec((None, None, d_ff, d), pick, pipeline_mode=pl.Buffered(1)),
    ]
    args = [x, mod, g, w1, w3, w2]
    pre_widths = ()
    if pre is not None:
        ys, wo = pre
        pre_widths = tuple(y.shape[-1] for y in ys)
        for y in ys:
            in_specs.append(pl.BlockSpec((1, tm, y.shape[-1]), row))
            args.append(y)
        in_specs.append(_const_spec(wo.shape))
        args.append(wo)
    if final_w is not None:
        in_specs.append(_const_spec((1, d)))
        args.append(final_w.reshape(1, d))
    return pl.pallas_call(
        functools.partial(_ffn_kernel, k=k, pre_widths=pre_widths, final=final_w is not None),
        name="macaron_ffn",
        grid=(bsz, n // tm),
        in_specs=in_specs,
        out_specs=pl.BlockSpec((1, tm, d), row),
        out_shape=jax.ShapeDtypeStruct((bsz, n, d), F32),
        compiler_params=_cparams(("parallel", "parallel")),
    )(*args)


def _rope(u, cos, sin_signed, axis):
    w = u.shape[axis]
    idx = lax.broadcasted_iota(jnp.int32, u.shape, axis)
    partner = jnp.where(idx % 32 < 16, pltpu.roll(u, w - 16, axis), pltpu.roll(u, 16, axis))
    return u * cos + partner * sin_signed


def _proj_kernel(*refs, segments, use_rope, any_major):
    it = iter(refs)
    x_ref, mod_ref, g_ref, w_ref = (next(it) for _ in range(4))
    wt_ref = next(it) if any_major else None
    cos_ref, sin_ref, cost_ref, sint_ref = (next(it) if use_rope else None for _ in range(4))
    out_refs = list(it)
    x = x_ref[0]
    hn = (_rms(x, g_ref[1:2, :]) * (1.0 + mod_ref[0, 4:5, :]) + mod_ref[0, 3:4, :]).astype(BF16)
    off = 0
    toff = 0
    for o_ref, (width, scale, rope, tchunk) in zip(out_refs, segments):
        if tchunk:
            ut = lax.dot_general(wt_ref[toff:toff + width, :], hn, (((1,), (1,)), ((), ())),
                                 preferred_element_type=F32)
            if rope:
                ut = _rope(ut, cost_ref[...], sint_ref[...], 0)
            if scale != 1.0:
                ut = ut * scale
            ut = ut.astype(o_ref.dtype)
            for j in range(ut.shape[1] // tchunk):
                o_ref[0, j] = ut[:, j * tchunk:(j + 1) * tchunk]
            toff += width
        else:
            u = jnp.dot(hn, w_ref[:, off:off + width], preferred_element_type=F32)
            if rope:
                u = _rope(u, cos_ref[...], sin_ref[...], 1)
            if scale != 1.0:
                u = u * scale
            o_ref[0] = u.astype(o_ref.dtype)
        off += width


def _project(x, mod, mod_row, g, w, segments, rope_tables=None):
    bsz, n, d = x.shape
    tm = min(n, 512)
    row = lambda b, i: (b, i, 0)
    out_specs, out_shape, wt_parts = [], [], []
    off = 0
    for width, _, _, tchunk in segments:
        if tchunk:
            tchunk = min(tchunk, tm)
            out_specs.append(pl.BlockSpec((1, tm // tchunk, width, tchunk), lambda b, i: (b, i, 0, 0)))
            out_shape.append(jax.ShapeDtypeStruct((bsz, n // tchunk, width, tchunk), BF16))
            wt_parts.append(w[:, off:off + width].T)
        else:
            out_specs.append(pl.BlockSpec((1, tm, width), row))
            out_shape.append(jax.ShapeDtypeStruct((bsz, n, width), BF16))
        off += width
    segments = tuple((w_, s_, r_, min(t_, tm)) for w_, s_, r_, t_ in segments)
    in_specs = [
        pl.BlockSpec((1, tm, d), row),
        pl.BlockSpec((1, N_MOD, d), lambda b, i: (mod_row(b), 0, 0)),
        _const_spec(g.shape),
        _const_spec(w.shape),
    ]
    args = [x, mod, g, w]
    if wt_parts:
        wt = jnp.concatenate(wt_parts, axis=0)
        in_specs.append(_const_spec(wt.shape))
        args.append(wt)
    if rope_tables is not None:
        for t in rope_tables:
            in_specs.append(pl.BlockSpec((tm, t.shape[1]), lambda b, i: (i, 0)))
            args.append(t)
        for t in rope_tables:
            in_specs.append(pl.BlockSpec((t.shape[1], tm), lambda b, i: (0, i)))
            args.append(t.T)
    return pl.pallas_call(
        functools.partial(_proj_kernel, segments=segments, use_rope=rope_tables is not None,
                          any_major=bool(wt_parts)),
        name="mixer_projection",
        grid=(bsz, n // tm),
        in_specs=in_specs,
        out_specs=out_specs,
        out_shape=out_shape,
        compiler_params=_cparams(("parallel", "parallel")),
    )(*args)


def _rope_tables(n, reps):
    t = jnp.arange(n)
    nf = A_QK_DIM // 4
    inv_freq = ROPE_BASE ** (-jnp.arange(nf, dtype=F32) / nf)
    cols, sins = [], []
    for pos in (t // GRID_W, t % GRID_W):
        ang = pos.astype(F32)[:, None] * inv_freq
        c, s = jnp.cos(ang), jnp.sin(ang)
        cols += [c, c]
        sins += [-s, s]
    return jnp.tile(jnp.concatenate(cols, 1), (1, reps)), jnp.tile(jnp.concatenate(sins, 1), (1, reps))


def _exp_rows(s_ref, e_ref, rows, m):
    for r in range(0, rows, SCORE_ROWS):
        e_ref[r:r + SCORE_ROWS, :] = jnp.exp2(s_ref[r:r + SCORE_ROWS, :] - m).astype(BF16)


def _dattn_rows_kernel(*refs, n_src, lam_init):
    lam_ref, sw_ref, q_ref = refs[:3]
    k_refs = refs[3:3 + n_src]
    v_refs = refs[3 + n_src:3 + 2 * n_src]
    o_ref = refs[3 + 2 * n_src]
    q = q_ref[0]
    lane = lax.broadcasted_iota(jnp.int32, q.shape, 1)
    zero = jnp.zeros_like(q)
    nt = (((1,), (1,)), ((), ()))
    qms = (jnp.where(lane < A_QK_DIM, q, zero), jnp.where(lane >= A_QK_DIM, q, zero))
    scores = [[lax.dot_general(qm, k_ref[0], nt, preferred_element_type=F32) for k_ref in k_refs] for qm in qms]
    outs = []
    for ss in scores:
        m = None
        for s in ss:
            ms = jnp.max(s, axis=-1, keepdims=True)
            m = ms if m is None else jnp.maximum(m, ms)
        acc = None
        den = None
        for s, v_ref in zip(ss, v_refs):
            e = jnp.exp2(s - m)
            ds = jnp.sum(e, axis=-1, keepdims=True)
            t = jnp.dot(e.astype(BF16), v_ref[0], preferred_element_type=F32)
            den = ds if den is None else den + ds
            acc = t if acc is None else acc + t
        outs.append(acc / den)
    lp = lam_ref[...]
    lam = (jnp.exp(jnp.sum(lp[0:1] * lp[1:2], axis=-1, keepdims=True))
           - jnp.exp(jnp.sum(lp[2:3] * lp[3:4], axis=-1, keepdims=True)) + lam_init)
    o = outs[0] - lam * outs[1]
    o_ref[0] = (_rms(o, sw_ref[...]) * (1.0 - lam_init)).astype(o_ref.dtype)


def _diff_attention_rows(q, ks, vs, lam_p, subln_w, lam_init):
    bsz, nq, width = q.shape
    tq = min(nq, 512)
    hb = lambda b, h, i: (b, i, h)
    kv = lambda b, h, i: (b, 0, h)
    in_specs = [
        pl.BlockSpec(lam_p.shape, lambda b, h, i: (0, 0)),
        pl.BlockSpec((1, A_V_DIM), lambda b, h, i: (0, 0)),
        pl.BlockSpec((1, tq, LANES), hb),
    ]
    in_specs += [pl.BlockSpec((1, k.shape[1], LANES), kv) for k in ks]
    in_specs += [pl.BlockSpec((1, v.shape[1], LANES), kv) for v in vs]
    return pl.pallas_call(
        functools.partial(_dattn_rows_kernel, n_src=len(ks), lam_init=lam_init),
        name="diff_attention_rows",
        grid=(bsz, A_HEADS, nq // tq),
        in_specs=in_specs,
        out_specs=pl.BlockSpec((1, tq, LANES), hb),
        out_shape=jax.ShapeDtypeStruct((bsz, nq, width), BF16),
        compiler_params=_cparams(("parallel", "parallel", "parallel")),
    )(lam_p, subln_w.reshape(1, A_V_DIM), q, *ks, *vs)


def _na_kernel(qt_ref, k_ref, vt_ref, kc_ref, vct_ref, tb_ref, o_ref, s_ref, e_ref, *, rows):
    i = pl.program_id(2)
    r0 = i * NA_QROWS
    start = jnp.clip(r0 - NA_ROWS // 2, 0, rows - NA_KROWS)
    koff = pl.multiple_of(start * GRID_W, 2 * GRID_W)
    kw = k_ref[0, pl.ds(koff, NA_KROWS * GRID_W), :]
    c0 = start // 2
    vwt = jnp.concatenate([vt_ref[0, c0 + j] for j in range(NA_KROWS // 2)], axis=1)
    kc = kc_ref[0]
    vct = jnp.concatenate([vct_ref[0, j] for j in range(vct_ref.shape[1])], axis=1)
    qt = qt_ref[0, 0]
    row = lax.broadcasted_iota(jnp.int32, qt.shape, 0)
    zero = jnp.zeros_like(qt)
    lane_b = lax.broadcasted_iota(jnp.int32, (GRID_W, LANES), 1)
    n_lat, n_all = kw.shape[0], kw.shape[0] + kc.shape[0]
    vt_all = jnp.concatenate([vwt, vct], axis=1)
    ones = jnp.ones((2 * SUBLANES, n_all), BF16)
    qms = (jnp.where(row < C_HEAD_DIM, qt, zero), jnp.where(row >= C_HEAD_DIM, qt, zero))
    for hh, qm in enumerate(qms):
        s_ref[hh, 0:n_lat, :] = jnp.dot(kw, qm, preferred_element_type=F32)
        s_ref[hh, n_lat:n_all, :] = jnp.dot(kc, qm, preferred_element_type=F32)
    outs = []
    for hh in range(2):
        mx = jnp.full((1, qt.shape[1]), NEG, F32)
        for m in range(NA_KROWS):
            kr = start + m
            blocks = []
            for jp in range(NA_QROWS // 2):
                idx = []
                for j in (2 * jp, 2 * jp + 1):
                    r = r0 + j
                    rs = jnp.clip(r - NA_ROWS // 2, 0, rows - NA_ROWS)
                    valid = jnp.logical_and(kr >= rs, kr < rs + NA_ROWS)
                    idx.append(jnp.where(valid, kr - r + NA_ROWS - 1, 2 * NA_ROWS - 1))
                blocks.append(jnp.where(lane_b < GRID_W, tb_ref[hh, idx[0]], tb_ref[hh, idx[1]]))
            blk = s_ref[hh, m * GRID_W:(m + 1) * GRID_W, :] + jnp.concatenate(blocks, axis=1)
            s_ref[hh, m * GRID_W:(m + 1) * GRID_W, :] = blk
            mx = jnp.maximum(mx, jnp.max(blk, axis=0, keepdims=True))
        for r in range(n_lat, n_all, SCORE_ROWS):
            mx = jnp.maximum(mx, jnp.max(s_ref[hh, r:r + SCORE_ROWS, :], axis=0, keepdims=True))
        _exp_rows(s_ref.at[hh], e_ref.at[hh], n_all, mx)
        e = e_ref[hh]
        acc = jnp.dot(vt_all, e, preferred_element_type=F32)
        l = jnp.dot(ones, e, preferred_element_type=F32)
        outs.append(acc / l[0:1])
    o_ref[0] = jnp.where(row < C_HEAD_DIM, outs[0], outs[1]).T.astype(o_ref.dtype)


def _na_rows_kernel(q_ref, k_ref, v_ref, kc_ref, vc_ref, tb_ref, o_ref, *, rows):
    i = pl.program_id(2)
    r0 = i * NA_QROWS
    start = jnp.clip(r0 - NA_ROWS // 2, 0, rows - NA_KROWS)
    koff = pl.multiple_of(start * GRID_W, GRID_W)
    kw = k_ref[0, pl.ds(koff, NA_KROWS * GRID_W), :]
    vw = v_ref[0, pl.ds(koff, NA_KROWS * GRID_W), :]
    kc = kc_ref[0]
    vc = vc_ref[0]
    q = q_ref[0]
    lane = lax.broadcasted_iota(jnp.int32, q.shape, 1)
    zero = jnp.zeros_like(q)
    lane_b = lax.broadcasted_iota(jnp.int32, (GRID_W, LANES), 1)
    nt = (((1,), (1,)), ((), ()))
    qms = (jnp.where(lane < C_HEAD_DIM, q, zero), jnp.where(lane >= C_HEAD_DIM, q, zero))
    scores = [(lax.dot_general(qm, kw, nt, preferred_element_type=F32),
               lax.dot_general(qm, kc, nt, preferred_element_type=F32)) for qm in qms]
    outs = []
    for hh, (s_lat, s_ctx) in enumerate(scores):
        bias_rows = []
        for j in range(NA_QROWS):
            r = r0 + j
            rs = jnp.clip(r - NA_ROWS // 2, 0, rows - NA_ROWS)
            blocks = []
            for mp in range(NA_KROWS // 2):
                idx = []
                for m in (2 * mp, 2 * mp + 1):
                    kr = start + m
                    valid = jnp.logical_and(kr >= rs, kr < rs + NA_ROWS)
                    idx.append(jnp.where(valid, kr - r + NA_ROWS - 1, 2 * NA_ROWS - 1))
                blocks.append(jnp.where(lane_b < GRID_W, tb_ref[hh, idx[0]], tb_ref[hh, idx[1]]))
            bias_rows.append(jnp.concatenate(blocks, axis=1))
        s_lat = s_lat + jnp.concatenate(bias_rows, axis=0)
        m = jnp.maximum(jnp.max(s_lat, axis=-1, keepdims=True), jnp.max(s_ctx, axis=-1, keepdims=True))
        e_lat = jnp.exp2(s_lat - m)
        e_ctx = jnp.exp2(s_ctx - m)
        denom = jnp.sum(e_lat, axis=-1, keepdims=True) + jnp.sum(e_ctx, axis=-1, keepdims=True)
        o = (jnp.dot(e_lat.astype(BF16), vw, preferred_element_type=F32)
             + jnp.dot(e_ctx.astype(BF16), vc, preferred_element_type=F32))
        outs.append(o / denom)
    o_ref[0] = jnp.where(lane < C_HEAD_DIM, outs[0], outs[1]).astype(o_ref.dtype)


def _neighbourhood_attention_rows(q, k, v, k_ctx, v_ctx, rpb):
    bsz, n, width = q.shape
    rows = n // GRID_W
    assert rows >= NA_KROWS and rows % NA_QROWS == 0
    n_ctx = k_ctx.shape[1]
    tq = NA_QROWS * GRID_W
    tb = _na_bias_table(rpb, key_major=False)
    kv = lambda b, h, i: (b, 0, h)
    return pl.pallas_call(
        functools.partial(_na_rows_kernel, rows=rows),
        name="neighbourhood_attention_rows",
        grid=(bsz, C_HEADS // 2, n // tq),
        in_specs=[
            pl.BlockSpec((1, tq, LANES), lambda b, h, i: (b, i, h)),
            pl.BlockSpec((1, n, LANES), kv),
            pl.BlockSpec((1, n, LANES), kv),
            pl.BlockSpec((1, n_ctx, LANES), kv),
            pl.BlockSpec((1, n_ctx, LANES), kv),
            pl.BlockSpec((2, 2 * NA_ROWS, GRID_W, LANES), lambda b, h, i: (h, 0, 0, 0)),
        ],
        out_specs=pl.BlockSpec((1, tq, LANES), lambda b, h, i: (b, i, h)),
        out_shape=jax.ShapeDtypeStruct((bsz, n, width), BF16),
        compiler_params=_cparams(("parallel", "parallel", "arbitrary")),
    )(q, k, v, k_ctx, v_ctx, tb)


def _na_bias_table(rpb, key_major=True):
    nh = rpb.shape[0]
    c = jnp.arange(GRID_W)
    cs = jnp.clip(c - NA_COLS // 2, 0, GRID_W - NA_COLS)
    kc = jnp.arange(GRID_W)
    inside = (kc[None, :] >= cs[:, None]) & (kc[None, :] < cs[:, None] + NA_COLS)
    off = jnp.clip(kc[None, :] - c[:, None] + NA_COLS - 1, 0, 2 * NA_COLS - 2)
    tb = jnp.where(inside[None, None], rpb[:, :, off] * LOG2E, NEG)
    tb = jnp.concatenate([tb, jnp.full((nh, 1, GRID_W, GRID_W), NEG, F32)], axis=1)
    if key_major:
        tb = jnp.swapaxes(tb, -1, -2)
    return jnp.concatenate([tb, tb], axis=-1)


def _neighbourhood_attention(qt, k, vt, k_ctx, vt_ctx, rpb):
    bsz, n, width = k.shape
    rows = n // GRID_W
    assert rows >= NA_KROWS and rows % NA_QROWS == 0 and NA_QROWS % 2 == 0 and NA_ROWS % 4 == 0
    n_ctx = k_ctx.shape[1]
    tq = NA_QROWS * GRID_W
    n_keys = NA_KROWS * GRID_W + n_ctx
    per_chunk = qt.shape[3] // tq
    assert vt.shape[3] == 2 * GRID_W and per_chunk * tq == qt.shape[3]
    tb = _na_bias_table(rpb)
    kv = lambda b, h, i: (b, 0, h)
    kvt = lambda b, h, i: (b, 0, h, 0)
    return pl.pallas_call(
        functools.partial(_na_kernel, rows=rows),
        name="neighbourhood_attention",
        grid=(bsz, C_HEADS // 2, n // tq),
        in_specs=[
            pl.BlockSpec((1, 1, LANES, tq), lambda b, h, i: (b, i // per_chunk, h, i % per_chunk)),
            pl.BlockSpec((1, n, LANES), kv),
            pl.BlockSpec((1, vt.shape[1], LANES, vt.shape[3]), kvt),
            pl.BlockSpec((1, n_ctx, LANES), kv),
            pl.BlockSpec((1, vt_ctx.shape[1], LANES, vt_ctx.shape[3]), kvt),
            pl.BlockSpec((2, 2 * NA_ROWS, GRID_W, LANES), lambda b, h, i: (h, 0, 0, 0)),
        ],
        out_specs=pl.BlockSpec((1, tq, LANES), lambda b, h, i: (b, i, h)),
        out_shape=jax.ShapeDtypeStruct((bsz, n, width), BF16),
        scratch_shapes=[pltpu.VMEM((2, n_keys, tq), F32), pltpu.VMEM((2, n_keys, tq), BF16)],
        compiler_params=_cparams(("parallel", "parallel", "arbitrary")),
    )(qt, k, vt, k_ctx, vt_ctx, tb)


def _filter_kernel(z_ref, zr_ref, w0_ref, b0_ref, w1_ref, b1_ref, fr_ref, wf_ref, wb_ref, dl_ref,
                   first_ref, second_ref, *, n_inner, cw):
    fr = fr_ref[...]
    tt = z_ref.shape[0]
    row = lax.broadcasted_iota(jnp.int32, (tt, cw), 0) + pl.program_id(0) * tt
    for pos_ref, wo_ref, o_ref, drop_row0 in ((z_ref, wf_ref, first_ref, False), (zr_ref, wb_ref, second_ref, True)):
        z = pos_ref[...]
        hid = jnp.sin(fr * (jnp.dot(z, w0_ref[...], precision=HIGHEST, preferred_element_type=F32) + b0_ref[...]))
        for i in range(n_inner):
            hid = jnp.sin(fr * (jnp.dot(hid, w1_ref[i], precision=HIGHEST, preferred_element_type=F32)
                                + b1_ref[i:i + 1, :]))
        filt = jnp.dot(hid, wo_ref[...], precision=HIGHEST, preferred_element_type=F32)
        window = jnp.exp(-z[:, 0:1] * dl_ref[...])
        if drop_row0:
            window = jnp.where(row == 0, 0.0, window)
        for j in range(filt.shape[1] // cw):
            o_ref[:, j * cw:(j + 1) * cw] = filt[:, j * cw:(j + 1) * cw] * window


def _filter_features(pos, n):
    t = (pos.astype(F32) / (n - 1))[:, None]
    w = (2.0 * math.pi / n) * pos.astype(F32)[:, None]
    bands = jnp.linspace(1e-4, HY_BANDS - 1, HY_BANDS, dtype=F32)[None, :]
    z = jnp.concatenate([t, jnp.cos(bands * w), -jnp.sin(bands * w)], axis=-1)
    return jnp.pad(z, ((0, 0), (0, LANES - HY_EMB)))


def _hyena_filters(n, f_w0, f_b0, f_w1, f_b1, f_freq, f_wout, cw):
    pos = jnp.arange(n, dtype=jnp.int32)
    z = _filter_features(pos, n)
    zr = _filter_features((n - pos) % n, n)
    w0 = jnp.pad(f_w0, ((0, LANES - HY_EMB), (0, 0)))
    deltas = jnp.abs(jnp.linspace(HY_MIN_DECAY, HY_MAX_DECAY, cw, dtype=F32))[None, :]
    hid = f_w0.shape[1]
    orders = f_wout.shape[1] // (2 * cw)
    wo = f_wout.reshape(hid, orders, 2, cw)
    w_fwd = wo[:, :, 0].reshape(hid, orders * cw)
    w_bwd = wo[:, :, 1].reshape(hid, orders * cw)
    tt = min(n, 512)
    full = lambda shape: pl.BlockSpec(shape, lambda i: (0,) * len(shape))
    out_spec = pl.BlockSpec((tt, orders * cw), lambda i: (i, 0))
    out_shape = jax.ShapeDtypeStruct((n, orders * cw), F32)
    return pl.pallas_call(
        functools.partial(_filter_kernel, n_inner=f_w1.shape[0], cw=cw),
        name="hyena_filters",
        grid=(n // tt,),
        in_specs=[
            pl.BlockSpec((tt, LANES), lambda i: (i, 0)),
            pl.BlockSpec((tt, LANES), lambda i: (i, 0)),
            full(w0.shape), full((1, hid)), full(f_w1.shape), full(f_b1.shape), full((1, hid)),
            full(w_fwd.shape), full(w_bwd.shape), full((1, cw)),
        ],
        out_specs=[out_spec, out_spec],
        out_shape=[out_shape, out_shape],
        compiler_params=_cparams(("parallel",)),
    )(z, zr, w0, f_b0.reshape(1, hid), f_w1, f_b1, f_freq.reshape(1, hid), w_fwd, w_bwd, deltas)


def _cis(phase, period, sign):
    ang = (2.0 * math.pi / period) * (phase % period).astype(F32)
    return jnp.cos(ang), sign * jnp.sin(ang)


def _cblock(mr, mi):
    return jnp.concatenate([jnp.concatenate([mr, -mi], -1), jnp.concatenate([mi, mr], -1)], -2)


def _dft_tables(n):
    big = 2 * n
    nb = big // DFT_L
    f = jnp.arange(DFT_L, dtype=jnp.int32)
    k1 = jnp.arange(nb, dtype=jnp.int32)
    blk = jnp.arange(nb, dtype=jnp.int32)
    ph1 = f[:, None, None] * k1[None, :, None] + DFT_L * blk[None, None, :] * k1[None, :, None]
    cr, ci = _cis(ph1, big, -1.0)
    t1_data = _cblock(cr[:, :, :nb // 2], ci[:, :, :nb // 2])
    t1_real = jnp.concatenate([cr, ci], axis=1)
    ph2 = f[:, None] * f[None, :]
    fr, fi = _cis(ph2, DFT_L, -1.0)
    t2 = _cblock(fr, fi)
    t2_inv = _cblock(fr, -fi)
    n1 = jnp.arange(nb // 2, dtype=jnp.int32)
    ph3 = DFT_L * n1[None, :, None] * k1[None, None, :] + f[:, None, None] * k1[None, None, :]
    dr, di = _cis(ph3, big, 1.0)
    t3 = _cblock(dr, di) / big
    return t1_data.astype(BF16), t1_real.astype(BF16), t2.astype(BF16), t2_inv.astype(BF16), t3.astype(BF16)


def _dft_stage1(xa_ref, xb_ref, t1_ref, a_ref, nb):
    half = nb // 2

    def body(f, carry):
        rhs = jnp.concatenate([xa_ref[pl.ds(f, half, stride=DFT_L), :],
                               xb_ref[pl.ds(f, half, stride=DFT_L), :]], axis=0).astype(BF16)
        out = jnp.dot(t1_ref[f], rhs, preferred_element_type=F32)
        a_ref[pl.ds(pl.multiple_of(f * P1, SUBLANES), 2 * nb), :] = out
        return carry

    lax.fori_loop(0, DFT_L, body, 0, unroll=SMALL_LOOP_UNROLL)


def _dft_stage2_in(a_ref, k1, nb):
    return jnp.concatenate([a_ref[pl.ds(k1, DFT_L, stride=P1), :],
                            a_ref[pl.ds(nb + k1, DFT_L, stride=P1), :]], axis=0).astype(BF16)


def _spectrum_kernel(xa_ref, xb_ref, t1_ref, t2_ref, o_ref, a_ref, *, nb):
    _dft_stage1(xa_ref, xb_ref, t1_ref, a_ref, nb)

    def body(k1, carry):
        out = jnp.dot(t2_ref[...], _dft_stage2_in(a_ref, k1, nb), preferred_element_type=F32)
        o_ref[pl.ds(pl.multiple_of(k1 * 2 * DFT_L, 2 * DFT_L), 2 * DFT_L), :] = out
        return carry

    lax.fori_loop(0, nb, body, 0, unroll=LOOP_UNROLL)


def _filter_spectrum(first, second, t1_real, t2):
    n, cols = first.shape
    nb = 2 * n // DFT_L
    col = lambda j: (0, j)
    return pl.pallas_call(
        functools.partial(_spectrum_kernel, nb=nb),
        name="filter_spectrum",
        grid=(cols // LANES,),
        in_specs=[
            pl.BlockSpec((n, LANES), col),
            pl.BlockSpec((n, LANES), col),
            _const_spec(t1_real.shape),
            _const_spec(t2.shape),
        ],
        out_specs=pl.BlockSpec((4 * n, LANES), col),
        out_shape=jax.ShapeDtypeStruct((4 * n, cols), F32),
        scratch_shapes=[pltpu.VMEM((DFT_L * P1, LANES), F32)],
        compiler_params=_cparams(("parallel",)),
    )(first, second, t1_real, t2)


def _short_conv_kernel(u_ref, w_ref, b_ref, o_ref):
    n = u_ref.shape[1]
    blk = min(n, LANES)
    halo = 2 * SUBLANES
    w0, w1, w2 = w_ref[0:1, :], w_ref[1:2, :], w_ref[2:3, :]
    bands = {}
    for r0 in range(0, n, blk):
        lo, hi = max(r0 - halo, 0), min(r0 + blk + halo, n)
        key = (r0 - lo, hi - lo)
        if key not in bands:
            i = lax.broadcasted_iota(jnp.int32, (blk, hi - lo), 0) + (r0 - lo)
            j = lax.broadcasted_iota(jnp.int32, (blk, hi - lo), 1)
            bands[key] = tuple(jnp.where(j == i + d, 1.0, 0.0).astype(BF16) for d in (-1, 1))
        tp, tn = bands[key]
        win = u_ref[0, lo:hi, :]
        prev = jnp.dot(tp, win, preferred_element_type=F32)
        nxt = jnp.dot(tn, win, preferred_element_type=F32)
        cur = u_ref[0, r0:r0 + blk, :].astype(F32)
        o_ref[0, r0:r0 + blk, :] = (prev * w0 + cur * w1 + nxt * w2 + b_ref[...]).astype(o_ref.dtype)


def _short_conv(u, u_blk, w, b):
    bsz, n, _ = u.shape
    taps, ch = w.shape
    cb = 2 * LANES
    assert ch % cb == 0 and (u_blk * LANES) % cb == 0
    first = u_blk * LANES // cb
    return pl.pallas_call(
        _short_conv_kernel,
        name="short_conv",
        grid=(bsz, ch // cb),
        in_specs=[
            pl.BlockSpec((1, n, cb), lambda bi, j: (bi, 0, first + j)),
            pl.BlockSpec((taps, cb), lambda bi, j: (0, j)),
            pl.BlockSpec((1, cb), lambda bi, j: (0, j)),
        ],
        out_specs=pl.BlockSpec((1, n, cb), lambda bi, j: (bi, 0, j)),
        out_shape=jax.ShapeDtypeStruct((bsz, n, ch), BF16),
        compiler_params=_cparams(("parallel", "parallel")),
    )(u, w, b.reshape(1, ch))


def _cmul_store(s, g, rows):
    sr, si = s[:rows], s[rows:]
    gr, gi = g[:rows], g[rows:]
    return jnp.concatenate([sr * gr - si * gi, sr * gi + si * gr], 0).astype(BF16)


def _hyena_conv_kernel(gate_ref, v_ref, skip_ref, g_ref, t1_ref, t2_ref, t2i_ref, t3_ref, o_ref,
                       va_ref, vb_ref, a_ref, p_ref, y_ref, *, nb, p2, p3):
    half = nb // 2
    two_l = 2 * DFT_L
    va_ref[...] = v_ref[0].astype(F32)
    vb_ref[...] = v_ref[1].astype(F32)

    _dft_stage1(va_ref, vb_ref, t1_ref, a_ref, nb)

    def fwd2(k1, carry):
        s = jnp.dot(t2_ref[...], _dft_stage2_in(a_ref, k1, nb), preferred_element_type=F32)
        off = pl.multiple_of(k1 * two_l, two_l)
        p_ref[pl.ds(off, two_l), :] = _cmul_store(s, g_ref[pl.ds(off, two_l), :], DFT_L)
        return carry

    lax.fori_loop(0, nb, fwd2, 0, unroll=LOOP_UNROLL)

    def inv1(k1, carry):
        off = pl.multiple_of(k1 * two_l, two_l)
        out = jnp.dot(t2i_ref[...], p_ref[pl.ds(off, two_l), :], preferred_element_type=F32)
        a_ref[pl.ds(pl.multiple_of(k1 * p2, SUBLANES), two_l), :] = out
        return carry

    lax.fori_loop(0, nb, inv1, 0, unroll=LOOP_UNROLL)

    def inv2(n2, carry):
        rhs = jnp.concatenate([a_ref[pl.ds(n2, nb, stride=p2), :],
                               a_ref[pl.ds(DFT_L + n2, nb, stride=p2), :]], axis=0).astype(BF16)
        out = jnp.dot(t3_ref[n2], rhs, preferred_element_type=F32)
        y_ref[pl.ds(pl.multiple_of(n2 * p3, SUBLANES), nb), :] = out
        return carry

    lax.fori_loop(0, DFT_L, inv2, 0, unroll=SMALL_LOOP_UNROLL)

    skip = skip_ref[...]

    def epilogue(n1, carry):
        off = pl.multiple_of(n1 * DFT_L, DFT_L)
        for b, vs_ref in enumerate((va_ref, vb_ref)):
            y = y_ref[pl.ds(b * half + n1, DFT_L, stride=p3), :]
            gate = gate_ref[b, pl.ds(off, DFT_L), :].astype(F32)
            o_ref[b, pl.ds(off, DFT_L), :] = (gate * (y + skip * vs_ref[pl.ds(off, DFT_L), :])).astype(o_ref.dtype)
        return carry

    lax.fori_loop(0, half, epilogue, 0, unroll=LOOP_UNROLL)


def _pitch(rows):
    p = -(-rows // SUBLANES)
    if p % 2 == 0:
        p += 1
    return p * SUBLANES


def _conv_specs(n, gate_blk, v_blk, spec_rows, spec_blk):
    return [
        pl.BlockSpec((2, n, LANES), lambda j, p: (p, 0, gate_blk + j)),
        pl.BlockSpec((2, n, LANES), lambda j, p: (p, 0, v_blk + j)),
        pl.BlockSpec((1, LANES), lambda j, p: (0, j)),
        pl.BlockSpec((spec_rows, LANES), lambda j, p: (0, spec_blk + j), pipeline_mode=pl.Buffered(1)),
    ]


def _hyena_conv(gate, gate_blk, v, v_blk, skip, spec, spec_blk, tables):
    bsz, n, _ = v.shape
    cw = skip.shape[-1]
    out_shape = jax.ShapeDtypeStruct((bsz, n, cw), BF16)
    out_spec = pl.BlockSpec((2, n, LANES), lambda j, p: (p, 0, j))
    grid = (cw // LANES, bsz // 2)
    params = _cparams(("parallel", "arbitrary"))
    if len(tables) == 2:
        fwd, inv = tables
        return pl.pallas_call(
            _hyena_conv_small_kernel,
            name="long_conv_small",
            grid=grid,
            in_specs=_conv_specs(n, gate_blk, v_blk, 4 * n, spec_blk) + [_const_spec(fwd.shape), _const_spec(inv.shape)],
            out_specs=out_spec,
            out_shape=out_shape,
            compiler_params=params,
        )(gate, v, skip, spec, fwd, inv)
    nb = 2 * n // DFT_L
    t1, t2, t2i, t3 = tables
    p2, p3 = _pitch(2 * DFT_L), _pitch(nb)
    return pl.pallas_call(
        functools.partial(_hyena_conv_kernel, nb=nb, p2=p2, p3=p3),
        name="long_conv",
        grid=grid,
        in_specs=_conv_specs(n, gate_blk, v_blk, 4 * n, spec_blk) + [_const_spec(t.shape) for t in tables],
        out_specs=out_spec,
        out_shape=out_shape,
        scratch_shapes=[
            pltpu.VMEM((n, LANES), F32), pltpu.VMEM((n, LANES), F32),
            pltpu.VMEM((max(DFT_L * P1, nb * p2), LANES), F32),
            pltpu.VMEM((4 * n, LANES), BF16),
            pltpu.VMEM((DFT_L * p3, LANES), F32),
        ],
        compiler_params=params,
    )(gate, v, skip, spec, t1, t2, t2i, t3)


def _small_tables(n):
    big = 2 * n
    k = jnp.arange(big, dtype=jnp.int32)
    fr, fi = _cis(k[:, None] * k[None, :], big, -1.0)
    fwd = _cblock(fr[:, :n], fi[:, :n])
    real = jnp.concatenate([fr, fi], axis=0)
    inv = _cblock(fr[:n, :], -fi[:n, :]) / big
    return fwd.astype(BF16), real.astype(BF16), inv.astype(BF16)


def _spectrum_small_kernel(xa_ref, xb_ref, f_ref, o_ref):
    taps = jnp.concatenate([xa_ref[...], xb_ref[...]], axis=0).astype(BF16)
    o_ref[...] = jnp.dot(f_ref[...], taps, preferred_element_type=F32)


def _filter_spectrum_small(first, second, real):
    n, cols = first.shape
    col = lambda j: (0, j)
    return pl.pallas_call(
        _spectrum_small_kernel,
        name="filter_spectrum_small",
        grid=(cols // LANES,),
        in_specs=[pl.BlockSpec((n, LANES), col), pl.BlockSpec((n, LANES), col), _const_spec(real.shape)],
        out_specs=pl.BlockSpec((4 * n, LANES), col),
        out_shape=jax.ShapeDtypeStruct((4 * n, cols), F32),
        compiler_params=_cparams(("parallel",)),
    )(first, second, real)


def _hyena_conv_small_kernel(gate_ref, v_ref, skip_ref, g_ref, f_ref, i_ref, o_ref):
    n = v_ref.shape[1]
    rhs = jnp.concatenate([v_ref[0], v_ref[1]], axis=0)
    s = jnp.dot(f_ref[...], rhs, preferred_element_type=F32)
    y = jnp.dot(i_ref[...], _cmul_store(s, g_ref[...], 2 * n), preferred_element_type=F32)
    skip = skip_ref[...]
    for b in range(2):
        yb = y[b * n:(b + 1) * n] + skip * v_ref[b].astype(F32)
        o_ref[b] = (gate_ref[b].astype(F32) * yb).astype(o_ref.dtype)


def _hyena(u, u_blk, conv_w, conv_b, hy_b, spec, tables):
    cw = hy_b.shape[-1]
    cb = cw // LANES
    uc = _short_conv(u, u_blk, conv_w, conv_b)
    z = _hyena_conv(uc, 0, uc, 2 * cb, hy_b[0:1], spec, 0, tables)
    return _hyena_conv(uc, cb, z, 0, hy_b[1:2], spec, cb, tables)


def kernel(x, c, ctx, c_ctx, mod_w, mod_b, norm_w, ffn_w1, ffn_w3, ffn_w2, ab_w_in, ab_w_out, diff_lambda,
           diff_subln_w, hy_conv_w, hy_conv_b, hy_f_w0, hy_f_b0, hy_f_w1, hy_f_b1, hy_f_freq, hy_f_wout,
           hy_bias, na_w_in, na_w_out, na_rpb, final_norm_w):
    bsz, n, d = x.shape
    n_ctx = ctx.shape[1]
    depth = mod_w.shape[0]
    a_width = A_HEADS * A_V_DIM
    b_width = hy_bias.shape[-1]
    c_width = C_HEADS * C_HEAD_DIM
    qk_scale = A_QK_DIM ** -0.5 * LOG2E
    na_scale = C_HEAD_DIM ** -0.5 * LOG2E

    mod = _modulation(c, c_ctx, mod_w, mod_b)
    lat_row = lambda b: b
    ctx_row = lambda b: bsz
    w1, w3, w2 = ffn_w1.astype(BF16), ffn_w3.astype(BF16), ffn_w2.astype(BF16)

    lat, cx = x, ctx
    for layer in range(depth):
        need_ctx = layer < depth - 1
        last = layer == depth - 1
        m, g = mod[layer], norm_w[layer]
        i = layer // 2
        lat = _ffn(lat, m, lat_row, g, w1, w3, w2, (layer, 0), 0)
        cx = _ffn(cx, m, ctx_row, g, w1, w3, w2, (layer, 0), 0)
        if layer % 2 == 0:
            w_in = ab_w_in[i].astype(BF16)
            w_out = ab_w_out[i].astype(BF16)
            lam_init = 0.8 - 0.6 * math.exp(-0.3 * layer)
            seg = [(a_width, qk_scale, True, 0), (a_width, 1.0, True, 0), (a_width, 1.0, False, 0),
                   (3 * b_width, 1.0, False, 0)]
            q_l, k_l, v_l, u_l = _project(lat, m, lat_row, g, w_in, seg, _rope_tables(n, a_width // A_QK_DIM))
            seg_c = [(s[0], s[1], False, s[3]) for s in seg]
            if need_ctx:
                q_c, k_c, v_c, u_c = _project(cx, m, ctx_row, g, w_in, seg_c)
            else:
                k_c, v_c = _project(cx, m, ctx_row, g, w_in[:, a_width:3 * a_width], seg_c[1:3])
            a_l = _diff_attention_rows(q_l, [k_c, k_l], [v_c, v_l], diff_lambda[i], diff_subln_w[i], lam_init)

            def hy(u, length):
                first, second = _hyena_filters(length, hy_f_w0[i], hy_f_b0[i], hy_f_w1[i], hy_f_b1[i],
                                               hy_f_freq[i], hy_f_wout[i], b_width)
                if 2 * length // DFT_L < 2 * SUBLANES:
                    fwd, real, inv = _small_tables(length)
                    spec = _filter_spectrum_small(first, second, real)
                    tables = (fwd, inv)
                else:
                    t1d, t1r, t2, t2i, t3 = _dft_tables(length)
                    spec = _filter_spectrum(first, second, t1r, t2)
                    tables = (t1d, t2, t2i, t3)
                return _hyena(u, 0, hy_conv_w[i], hy_conv_b[i], hy_bias[i], spec, tables)

            b_l = hy(u_l, n)
            pre_l = ([a_l, b_l], w_out)
            if need_ctx:
                a_c = _diff_attention_rows(q_c, [k_c], [v_c], diff_lambda[i], diff_subln_w[i], lam_init)
                b_c = hy(u_c, n_ctx)
                pre_c = ([a_c, b_c], w_out)
        else:
            w_in = na_w_in[i].astype(BF16)
            w_out = na_w_out[i].astype(BF16)
            seg = [(c_width, na_scale, False, ATTN_CHUNK), (c_width, 1.0, False, 0), (c_width, 1.0, False, 2 * GRID_W)]
            q_l, k_l, v_l = _project(lat, m, lat_row, g, w_in, seg)
            if need_ctx:
                raise NotImplementedError("context outputs of the neighbourhood mixer are not needed at this depth")
            k_c, v_c = _project(cx, m, ctx_row, g, w_in[:, c_width:], seg[1:])
            o_l = _neighbourhood_attention(q_l, k_l, v_l, k_c, v_c, na_rpb[i])
            pre_l = ([o_l], w_out)
        lat = _ffn(lat, m, lat_row, g, w1, w3, w2, (layer, 1), 2, pre=pre_l,
                   final_w=final_norm_w if last else None)
        if need_ctx:
            cx = _ffn(cx, m, ctx_row, g, w1, w3, w2, (layer, 1), 2, pre=pre_c)
    return lat
```
